```python
import math
import jax, jax.numpy as jnp
from jax import lax
import numpy as np

D_MODEL = 1024
BATCH = 16
SEQ = 4096
DEPTH = 1
DEC_BATCH = 128
DEC_SEQ = 1
PAST_LEN = 8192
PAGE_SIZE = 128

N_META = 16
N_RET_HEADS = 4
RET_QK_DIM = 128
RET_V_DIM = D_MODEL // N_RET_HEADS
RET_CHUNK = 128
N_DIFF_HEADS = 8
DIFF_QK_DIM = D_MODEL // (2 * N_DIFF_HEADS)
DIFF_V_DIM = 2 * DIFF_QK_DIM
Q_BLOCK = 128
D_FF = -(-8 * D_MODEL // (3 * 256)) * 256
ROPE_BASE = 10000.0
RMS_EPS = 1e-6
SPLIT_SIZES = (N_RET_HEADS * RET_QK_DIM, N_RET_HEADS * RET_QK_DIM, N_RET_HEADS * RET_V_DIM, D_MODEL,
               N_DIFF_HEADS * 2 * DIFF_QK_DIM, N_DIFF_HEADS * 2 * DIFF_QK_DIM, N_DIFF_HEADS * DIFF_V_DIM,
               D_MODEL, D_MODEL)
D_IN_PROJ = sum(SPLIT_SIZES)

kernel_name = 'hybrid_retention_diffattn_decoder'


def rms_norm(x, gain):
    xf = x.astype(jnp.float32)
    y = xf * lax.rsqrt(jnp.mean(xf * xf, axis=-1, keepdims=True) + RMS_EPS)
    return (y * gain.astype(jnp.float32)).astype(x.dtype)


def head_rms(x):
    xf = x.astype(jnp.float32)
    return xf * lax.rsqrt(jnp.mean(xf * xf, axis=-1, keepdims=True) + RMS_EPS)


def rotary(x, pos):
    half = x.shape[-1] // 2
    freqs = ROPE_BASE ** (-jnp.arange(half, dtype=jnp.float32) / half)
    ang = pos[:, None] * freqs[None, :]
    cos = jnp.cos(ang)[:, None, :]
    sin = jnp.sin(ang)[:, None, :]
    xf = x.astype(jnp.float32)
    x1, x2 = xf[..., :half], xf[..., half:]
    return jnp.concatenate([x1 * cos - x2 * sin, x1 * sin + x2 * cos], axis=-1).astype(x.dtype)


def in_projections(h, pos, w_in):
    b, t = h.shape[0], h.shape[1]
    proj = jnp.einsum('btd,de->bte', h, w_in)
    cuts = np.cumsum(SPLIT_SIZES)[:-1].tolist()
    rq, rk, rv, rg, dq, dk, dv, ga, gb = jnp.split(proj, cuts, axis=-1)
    rq = rotary(rq.reshape(b, t, N_RET_HEADS, RET_QK_DIM), pos)
    rk = rotary(rk.reshape(b, t, N_RET_HEADS, RET_QK_DIM), pos) * (RET_QK_DIM ** -0.5)
    rv = rv.reshape(b, t, N_RET_HEADS, RET_V_DIM)
    dq = dq.reshape(b, t, N_DIFF_HEADS, 2, DIFF_QK_DIM)
    dk = dk.reshape(b, t, N_DIFF_HEADS, 2, DIFF_QK_DIM)
    dv = dv.reshape(b, t, N_DIFF_HEADS, DIFF_V_DIM)
    return rq, rk, rv, rg, dq, dk, dv, ga, gb


def ret_log_gamma():
    return jnp.log(1.0 - 2.0 ** (-5.0 - jnp.arange(N_RET_HEADS, dtype=jnp.float32)))


def retention_chunk(state, q, k, v, log_gamma):
    c = q.shape[1]
    qf, kf, vf = q.astype(jnp.float32), k.astype(jnp.float32), v.astype(jnp.float32)
    i = jnp.arange(c, dtype=jnp.float32)
    diff = i[:, None] - i[None, :]
    decay = jnp.where(diff >= 0, jnp.exp(jnp.maximum(diff, 0.0)[None] * log_gamma[:, None, None]), 0.0)
    scores = jnp.einsum('bihd,bjhd->bhij', qf, kf) * decay[None]
    o = jnp.einsum('bhij,bjhe->bihe', scores, vf)
    q_decay = jnp.exp((i + 1.0)[:, None] * log_gamma[None, :])
    o = o + jnp.einsum('bihd,bhde->bihe', qf, state) * q_decay[None, :, :, None]
    k_decay = jnp.exp((c - 1.0 - i)[:, None] * log_gamma[None, :])
    new_state = state * jnp.exp(c * log_gamma)[None, :, None, None] + jnp.einsum(
        'bjhd,bjhe->bhde', kf * k_decay[None, :, :, None], vf)
    return o, new_state


def retention_prompt(q, k, v, log_gamma):
    b = q.shape[0]
    s0 = jnp.zeros((b, N_RET_HEADS, RET_QK_DIM, RET_V_DIM), jnp.float32)
    o_meta, s = retention_chunk(s0, q[:, :N_META], k[:, :N_META], v[:, :N_META], log_gamma)
    n_chunks = (q.shape[1] - N_META) // RET_CHUNK

    def chunks(a):
        return a[:, N_META:].reshape((b, n_chunks, RET_CHUNK) + a.shape[2:]).swapaxes(0, 1)

    def step(carry, xs):
        o, carry = retention_chunk(carry, xs[0], xs[1], xs[2], log_gamma)
        return carry, o

    s, o_real = lax.scan(step, s, (chunks(q), chunks(k), chunks(v)))
    o_real = o_real.swapaxes(0, 1).reshape((b, n_chunks * RET_CHUNK) + o_real.shape[3:])
    return jnp.concatenate([o_meta, o_real], axis=1), s


def diff_lambda(lq1, lk1, lq2, lk2, lam_init):
    f = jnp.float32
    return (jnp.exp(jnp.sum(lq1.astype(f) * lk1.astype(f)))
            - jnp.exp(jnp.sum(lq2.astype(f) * lk2.astype(f))) + lam_init)


def diff_attn_prompt(q, k, v, lam):
    b, seq_len = q.shape[0], q.shape[1]
    scale = DIFF_QK_DIM ** -0.5
    kf = k.astype(jnp.float32)
    vf = v.astype(jnp.float32)
    k_pos = jnp.arange(seq_len)

    def attend(qb, q_pos):
        s = jnp.einsum('bqhmd,bkhmd->bhmqk', qb.astype(jnp.float32), kf) * scale
        s = jnp.where(k_pos[None, :] <= q_pos[:, None], s, -jnp.inf)
        p = jax.nn.softmax(s, axis=-1)
        o = jnp.einsum('bhmqk,bkhe->bqhme', p, vf)
        return o[..., 0, :] - lam * o[..., 1, :]

    o_meta = attend(q[:, :N_META], jnp.arange(N_META))
    n_blk = (seq_len - N_META) // Q_BLOCK
    qb = q[:, N_META:].reshape(b, n_blk, Q_BLOCK, N_DIFF_HEADS, 2, DIFF_QK_DIM).swapaxes(0, 1)
    starts = N_META + Q_BLOCK * jnp.arange(n_blk)
    o_blk = lax.map(lambda a: attend(a[0], a[1] + jnp.arange(Q_BLOCK)), (qb, starts))
    o_real = o_blk.swapaxes(0, 1).reshape(b, n_blk * Q_BLOCK, N_DIFF_HEADS, DIFF_V_DIM)
    return jnp.concatenate([o_meta, o_real], axis=1)


def diff_attn_sample(q, k_new, v_new, cache_k, cache_v, layer, page_table, lam):
    db, t = q.shape[0], q.shape[1]
    scale = DIFF_QK_DIM ** -0.5
    qf = q.astype(jnp.float32)
    s = jnp.einsum('bqhmd,bkhmd->bhmqk', qf, k_new.astype(jnp.float32)) * scale
    s = jnp.where(jnp.tril(jnp.ones((t, t), bool)), s, -jnp.inf)
    m = jnp.max(s, axis=-1)
    p = jnp.exp(s - m[..., None])
    l = jnp.sum(p, axis=-1)
    acc = jnp.einsum('bhmqk,bkhe->bhmqe', p, v_new.astype(jnp.float32))

    def page_step(carry, pids):
        m, l, acc = carry
        kp = cache_k[layer, pids].reshape(db, PAGE_SIZE, N_DIFF_HEADS, 2, DIFF_QK_DIM).astype(jnp.float32)
        vp = cache_v[layer, pids].astype(jnp.float32)
        sp = jnp.einsum('bqhmd,bkhmd->bhmqk', qf, kp) * scale
        m_new = jnp.maximum(m, jnp.max(sp, axis=-1))
        corr = jnp.exp(m - m_new)
        pp = jnp.exp(sp - m_new[..., None])
        l = l * corr + jnp.sum(pp, axis=-1)
        acc = acc * corr[..., None] + jnp.einsum('bhmqk,bkhe->bhmqe', pp, vp)
        return (m_new, l, acc), None

    (m, l, acc), _ = lax.scan(page_step, (m, l, acc), page_table.T)
    o = (acc / l[..., None]).transpose(0, 3, 1, 2, 4)
    return o[..., 0, :] - lam * o[..., 1, :]


def swiglu(h, w_gate, w_up, w_down):
    a = jnp.einsum('btd,df->btf', h, w_gate)
    u = jnp.einsum('btd,df->btf', h, w_up)
    return jnp.einsum('btf,fd->btd', jax.nn.silu(a) * u, w_down)


def decoder_layer(x, pos, retention_fn, diff_fn, lam_init, norm_mix_pre, norm_mix_post, w_in,
                  norm_diff_head, w_out, norm_ffn_pre, norm_ffn_post, w_gate, w_up, w_down):
    b, t = x.shape[0], x.shape[1]
    f = jnp.float32
    h = rms_norm(x, norm_mix_pre)
    rq, rk, rv, rg, dq, dk, dv, ga, gb = in_projections(h, pos, w_in)
    o_ret, ret_state = retention_fn(rq, rk, rv)
    o_diff = diff_fn(dq, dk, dv)
    y_ret = head_rms(o_ret).reshape(b, t, D_MODEL) * jax.nn.silu(rg.astype(f))
    y_diff = (head_rms(o_diff) * norm_diff_head.astype(f) * (1.0 - lam_init)).reshape(b, t, D_MODEL)
    merged = jax.nn.sigmoid(ga.astype(f)) * y_ret + jax.nn.sigmoid(gb.astype(f)) * y_diff
    mix = jnp.einsum('btd,de->bte', merged.astype(x.dtype), w_out)
    x = x + rms_norm(mix, norm_mix_post)
    ff = swiglu(rms_norm(x, norm_ffn_pre), w_gate, w_up, w_down)
    x = x + rms_norm(ff, norm_ffn_post)
    return x, dk.reshape(b, t, N_DIFF_HEADS, 2 * DIFF_QK_DIM), dv, ret_state


def setup_inputs(seed: int = 0) -> dict:
    key = jax.random.key(seed)
    ks = jax.random.split(key, 24)
    n_pages = PAST_LEN // PAGE_SIZE
    n_pool = (5 * DEC_BATCH * n_pages) // 4

    def nrm(k, shape, scale):
        return scale * jax.random.normal(k, shape, jnp.float32)

    def gain(k, width):
        return 1.0 + nrm(k, (DEPTH, width), 0.02)

    page_table = jax.random.permutation(ks[5], n_pool)[:DEC_BATCH * n_pages].reshape(
        DEC_BATCH, n_pages).astype(jnp.int32)
    return {
        'x_prompt': nrm(ks[0], (BATCH, SEQ, D_MODEL), 1.0),
        'x_sample': nrm(ks[1], (DEC_BATCH, DEC_SEQ, D_MODEL), 1.0),
        'cache_k': nrm(ks[2], (DEPTH, n_pool, PAGE_SIZE, N_DIFF_HEADS, 2 * DIFF_QK_DIM), 1.0),
        'cache_v': nrm(ks[3], (DEPTH, n_pool, PAGE_SIZE, N_DIFF_HEADS, DIFF_V_DIM), 1.0),
        'state_ret': nrm(ks[4], (DEPTH, DEC_BATCH, N_RET_HEADS, RET_QK_DIM, RET_V_DIM), 0.5),
        'page_table': page_table,
        'meta_tokens': nrm(ks[6], (N_META, D_MODEL), 1.0),
        'norm_mix_pre': gain(ks[7], D_MODEL),
        'norm_mix_post': gain(ks[8], D_MODEL),
        'w_in': nrm(ks[9], (DEPTH, D_MODEL, D_IN_PROJ), D_MODEL ** -0.5),
        'lambda_q1': nrm(ks[10], (DEPTH, DIFF_QK_DIM), 0.1),
        'lambda_k1': nrm(ks[11], (DEPTH, DIFF_QK_DIM), 0.1),
        'lambda_q2': nrm(ks[12], (DEPTH, DIFF_QK_DIM), 0.1),
        'lambda_k2': nrm(ks[13], (DEPTH, DIFF_QK_DIM), 0.1),
        'norm_diff_head': gain(ks[14], DIFF_V_DIM),
        'w_out': nrm(ks[15], (DEPTH, D_MODEL, D_MODEL), D_MODEL ** -0.5),
        'norm_ffn_pre': gain(ks[16], D_MODEL),
        'norm_ffn_post': gain(ks[17], D_MODEL),
        'w_gate': nrm(ks[18], (DEPTH, D_MODEL, D_FF), D_MODEL ** -0.5),
        'w_up': nrm(ks[19], (DEPTH, D_MODEL, D_FF), D_MODEL ** -0.5),
        'w_down': nrm(ks[20], (DEPTH, D_FF, D_MODEL), D_FF ** -0.5),
    }


def reference(x_prompt, x_sample, cache_k, cache_v, state_ret, page_table, meta_tokens,
              norm_mix_pre, norm_mix_post, w_in, lambda_q1, lambda_k1, lambda_q2, lambda_k2,
              norm_diff_head, w_out, norm_ffn_pre, norm_ffn_post, w_gate, w_up, w_down):
    b = x_prompt.shape[0]
    t_s = x_sample.shape[1]
    meta = jnp.broadcast_to(meta_tokens[None].astype(x_prompt.dtype), (b, N_META, D_MODEL))
    xp = jnp.concatenate([meta, x_prompt], axis=1)
    xs = x_sample
    pos_p = jnp.arange(xp.shape[1], dtype=jnp.float32)
    pos_s = PAST_LEN + jnp.arange(t_s, dtype=jnp.float32)
    log_gamma = ret_log_gamma()
    kp_rows, vp_rows, sp_rows, ks_rows, vs_rows, ss_rows = [], [], [], [], [], []
    for layer in range(DEPTH):
        lam_init = 0.8 - 0.6 * math.exp(-0.3 * layer)
        lam = diff_lambda(lambda_q1[layer], lambda_k1[layer], lambda_q2[layer], lambda_k2[layer], lam_init)
        weights = (norm_mix_pre[layer], norm_mix_post[layer], w_in[layer], norm_diff_head[layer],
                   w_out[layer], norm_ffn_pre[layer], norm_ffn_post[layer], w_gate[layer],
                   w_up[layer], w_down[layer])
        xp, k_p, v_p, s_p = decoder_layer(
            xp, pos_p,
            lambda q, k, v: retention_prompt(q, k, v, log_gamma),
            lambda q, k, v: diff_attn_prompt(q, k, v, lam),
            lam_init, *weights)
        ret0 = state_ret[layer].astype(jnp.float32)
        xs, k_s, v_s, s_s = decoder_layer(
            xs, pos_s,
            lambda q, k, v: retention_chunk(ret0, q, k, v, log_gamma),
            lambda q, k, v: diff_attn_sample(q, k, v, cache_k, cache_v, layer, page_table, lam),
            lam_init, *weights)
        kp_rows.append(k_p)
        vp_rows.append(v_p)
        sp_rows.append(s_p)
        ks_rows.append(k_s)
        vs_rows.append(v_s)
        ss_rows.append(s_s)
    y_prompt = xp[:, N_META:]
    y_sample = xs
    k_prompt = jnp.stack(kp_rows)
    v_prompt = jnp.stack(vp_rows)
    ret_prompt = jnp.stack(sp_rows)
    k_sample = jnp.stack(ks_rows)
    v_sample = jnp.stack(vs_rows)
    ret_sample = jnp.stack(ss_rows)
    return (y_prompt, y_sample, k_prompt, v_prompt, ret_prompt, k_sample, v_sample, ret_sample)
```

```python
import functools

import jax
import jax.numpy as jnp
from jax import lax
from jax.experimental import pallas as pl
from jax.experimental.pallas import tpu as pltpu

F32 = jnp.float32
BF16 = jnp.bfloat16

D_MODEL = 1024
N_META = 16
N_RET_HEADS = 4
RET_QK_DIM = 128
RET_V_DIM = 256
N_DIFF_HEADS = 8
DIFF_QK_DIM = 64
DIFF_V_DIM = 128
PAGE_SIZE = 128
PAST_LEN = 8192
ROPE_BASE = 10000.0
RMS_EPS = 1e-6
LAM_INIT = 0.8 - 0.6 * 1.0

C_RQ, C_RK, C_RV, C_RG, C_DQ, C_DK, C_DV, C_GA, C_GB = 0, 512, 1024, 2048, 3072, 4096, 5120, 6144, 7168
D_IN_PROJ = 8192

META_PAD = 128
RET_CHUNK = 256
Q_TILE = 256
Q_SUB = 128
KV_BLOCK = 256
PAGES_PER_STEP = 8
VMEM_LIMIT = 56 * 1024 * 1024


def _rms(x):
    return x * lax.rsqrt(jnp.mean(x * x, axis=-1, keepdims=True) + RMS_EPS)


def _sigmoid(x):
    return 1.0 / (1.0 + jnp.exp(-x))


def _rope(p, cos, sin, n_heads):
    parts = []
    for hh in range(n_heads):
        r = p[:, hh * 128:(hh + 1) * 128]
        parts.append(r * cos + pltpu.roll(r, 64, 1) * sin)
    return parts


def _lambda(lq1_ref, lk1_ref, lq2_ref, lk2_ref):
    a = jnp.sum(lq1_ref[...] * lk1_ref[...], axis=-1, keepdims=True)
    b = jnp.sum(lq2_ref[...] * lk2_ref[...], axis=-1, keepdims=True)
    return jnp.exp(a) - jnp.exp(b) + LAM_INIT


def _inproj_prompt_kernel(x_ref, g_ref, w_ref, cos_ref, sin_ref, ndh_ref,
                          rq_ref, rk_ref, rv_ref, gret_ref, qt_ref, kb_ref, vt_ref, kf_ref, vf_ref,
                          gdiff_ref, *, tm):
    h = (_rms(x_ref[0]) * g_ref[...]).astype(BF16)

    def proj(c0, n):
        return jnp.dot(h, w_ref[:, c0:c0 + n], preferred_element_type=F32)

    cos = cos_ref[...]
    sin = sin_ref[...]
    rq_ref[0] = jnp.concatenate(_rope(proj(C_RQ, 512), cos, sin, N_RET_HEADS), axis=1).astype(BF16)
    rk = jnp.concatenate(_rope(proj(C_RK, 512), cos, sin, N_RET_HEADS), axis=1)
    rk_ref[0] = (rk * (RET_QK_DIM ** -0.5)).astype(BF16)
    rv_ref[0] = proj(C_RV, 1024).astype(BF16)
    rg = proj(C_RG, 1024)
    ga = proj(C_GA, 1024)
    gret_ref[0] = rg * _sigmoid(rg) * _sigmoid(ga)
    dq = proj(C_DQ, 1024) * (DIFF_QK_DIM ** -0.5)
    for hh in range(N_DIFF_HEADS):
        for s in range(tm // Q_SUB):
            qt_ref[0, hh, s] = dq[s * Q_SUB:(s + 1) * Q_SUB, hh * 128:(hh + 1) * 128].T.astype(BF16)
    dk = proj(C_DK, 1024)
    kf_ref[0] = dk
    kb_ref[0] = dk.astype(BF16)
    dv = proj(C_DV, 1024)
    vf_ref[0] = dv
    for hh in range(N_DIFF_HEADS):
        for s in range(tm // KV_BLOCK):
            vt_ref[0, hh, s] = dv[s * KV_BLOCK:(s + 1) * KV_BLOCK, hh * 128:(hh + 1) * 128].T.astype(BF16)
    gb = proj(C_GB, 1024)
    gdiff_ref[0] = _sigmoid(gb) * ndh_ref[...]


def _meta_kernel(mt_ref, g_ref, wrk_ref, wrv_ref, wdk_ref, wdv_ref, cos_ref, sin_ref, kdec_ref,
                 kf_ref, vf_ref, kb_ref, vt_ref, s_ref):
    h = (_rms(mt_ref[...]) * g_ref[...]).astype(BF16)
    rk = _rope(jnp.dot(h, wrk_ref[...], preferred_element_type=F32), cos_ref[...], sin_ref[...], N_RET_HEADS)
    rv = jnp.dot(h, wrv_ref[...], preferred_element_type=F32).astype(BF16)
    dk = jnp.dot(h, wdk_ref[...], preferred_element_type=F32)
    dv = jnp.dot(h, wdv_ref[...], preferred_element_type=F32)
    kf_ref[...] = dk
    vf_ref[...] = dv
    kb_ref[...] = dk.astype(BF16)
    for hh in range(N_DIFF_HEADS):
        vt_ref[hh] = dv[:, hh * 128:(hh + 1) * 128].T.astype(BF16)
    for hh in range(N_RET_HEADS):
        kd = (rk[hh] * (RET_QK_DIM ** -0.5)) * kdec_ref[hh]
        s_ref[hh] = jnp.dot(kd.T.astype(BF16), rv[:, hh * RET_V_DIM:(hh + 1) * RET_V_DIM],
                            preferred_element_type=F32)


def _inproj_sample_kernel(x_ref, g_ref, w_ref, cos_ref, sin_ref, out_ref):
    j = pl.program_id(0)
    h = (_rms(x_ref[...]) * g_ref[...]).astype(BF16)
    p = jnp.dot(h, w_ref[...], preferred_element_type=F32)

    @pl.when(j == 0)
    def _():
        parts = _rope(p, cos_ref[...], sin_ref[...], 2 * N_RET_HEADS)
        parts = parts[:N_RET_HEADS] + [r * (RET_QK_DIM ** -0.5) for r in parts[N_RET_HEADS:]]
        out_ref[...] = jnp.concatenate(parts, axis=1)

    @pl.when(j != 0)
    def _():
        out_ref[...] = p


def _ret_prompt_kernel(rq_ref, rk_ref, rv_ref, gret_ref, smeta_ref, decay_ref, qdec_ref, kdec_ref, gc_ref,
                       y_ref, sout_ref, state_ref, *, n_chunks):
    c = pl.program_id(1)

    @pl.when(c == 0)
    def _():
        state_ref[...] = smeta_ref[...]

    for hh in range(N_RET_HEADS):
        q = rq_ref[0, :, hh * 128:(hh + 1) * 128]
        k = rk_ref[0, :, hh * 128:(hh + 1) * 128]
        v = rv_ref[0, :, hh * RET_V_DIM:(hh + 1) * RET_V_DIM]
        s_old = state_ref[hh]
        sc = lax.dot_general(q, k, (((1,), (1,)), ((), ())), preferred_element_type=F32) * decay_ref[hh]
        o = jnp.dot(sc.astype(BF16), v, preferred_element_type=F32)
        o = o + jnp.dot(q, s_old.astype(BF16), preferred_element_type=F32) * qdec_ref[hh]
        kd = k.astype(F32) * kdec_ref[hh]
        state_ref[hh] = s_old * gc_ref[hh] + jnp.dot(kd.T.astype(BF16), v, preferred_element_type=F32)
        y_ref[0, :, hh * RET_V_DIM:(hh + 1) * RET_V_DIM] = (
            _rms(o) * gret_ref[0, :, hh * RET_V_DIM:(hh + 1) * RET_V_DIM])

    @pl.when(c == n_chunks - 1)
    def _():
        sout_ref[0, 0] = state_ref[...]


def _ret_sample_kernel(rqk_ref, rv_ref, rg_ref, ga_ref, st_ref, gam_ref, y_ref, snew_ref, *, rows):
    eye = (lax.broadcasted_iota(jnp.int32, (128, 128), 0) == lax.broadcasted_iota(jnp.int32, (128, 128), 1))

    def column(r):
        return jnp.sum(jnp.where(eye, jnp.broadcast_to(r, (128, 128)), 0.0), axis=1, keepdims=True)

    for bl in range(rows):
        for hh in range(N_RET_HEADS):
            q = rqk_ref[bl:bl + 1, hh * 128:(hh + 1) * 128]
            k = rqk_ref[bl:bl + 1, 512 + hh * 128:512 + (hh + 1) * 128]
            cs = slice(hh * RET_V_DIM, (hh + 1) * RET_V_DIM)
            v = rv_ref[bl:bl + 1, cs]
            s_old = st_ref[0, bl, hh]
            gam = gam_ref[hh]
            score = jnp.sum(q * k, axis=1, keepdims=True)
            o = score * v + jnp.sum(column(q) * s_old, axis=0, keepdims=True) * gam
            snew_ref[0, bl, hh] = s_old * gam + column(k) * v
            rg = rg_ref[bl:bl + 1, cs]
            y_ref[bl:bl + 1, cs] = _rms(o) * (rg * _sigmoid(rg)) * _sigmoid(ga_ref[bl:bl + 1, cs])


def _diff_prompt_kernel(lq1_ref, lk1_ref, lq2_ref, lk2_ref, qt_ref, k_ref, vt_ref, km_ref, vtm_ref,
                        gdiff_ref, yret_ref, out_ref):
    qi = pl.program_id(2)
    lam = _lambda(lq1_ref, lk1_ref, lq2_ref, lk2_ref)
    wide = 2 * Q_SUB
    row = lax.broadcasted_iota(jnp.int32, (128, wide), 0)
    col = lax.broadcasted_iota(jnp.int32, (128, wide), 1)
    bd_mask = (row < DIFF_QK_DIM) == (col < Q_SUB)
    neg_inf = -jnp.inf

    def update(carry, s, vt):
        m, l, acc = carry
        m_new = jnp.maximum(m, jnp.max(s, axis=0, keepdims=True))
        alpha = jnp.exp(m - m_new)
        p = jnp.exp(s - m_new)
        l = alpha * l + jnp.sum(p, axis=0, keepdims=True)
        acc = acc * alpha + jnp.dot(vt, p.astype(BF16), preferred_element_type=F32)
        return m_new, l, acc

    for hh in range(2):
        hs = slice(hh * 128, (hh + 1) * 128)
        for st in range(Q_TILE // Q_SUB):
            qt = qt_ref[0, hh, st]
            qbd = jnp.where(bd_mask, jnp.concatenate([qt, qt], axis=1), jnp.zeros((), BF16))
            s = jnp.dot(km_ref[:, hs], qbd, preferred_element_type=F32)
            s = jnp.where(row < N_META, s, neg_inf)
            m = jnp.max(s, axis=0, keepdims=True)
            p = jnp.exp(s - m)
            l = jnp.sum(p, axis=0, keepdims=True)
            acc = jnp.dot(vtm_ref[hh], p.astype(BF16), preferred_element_type=F32)

            def body(j, carry, hs=hs, hh=hh, qbd=qbd):
                kb = k_ref[0, pl.ds(pl.multiple_of(j * KV_BLOCK, KV_BLOCK), KV_BLOCK), hs]
                s = jnp.dot(kb, qbd, preferred_element_type=F32)
                return update(carry, s, vt_ref[0, hh, j])

            carry = lax.fori_loop(0, qi, body, (m, l, acc))
            nk = Q_SUB * (st + 1)
            kb = k_ref[0, pl.ds(pl.multiple_of(qi * KV_BLOCK, KV_BLOCK), nk), hs]
            s = jnp.dot(kb, qbd, preferred_element_type=F32)
            kpos = lax.broadcasted_iota(jnp.int32, (nk, wide), 0)
            qpos = (lax.broadcasted_iota(jnp.int32, (nk, wide), 1) & (Q_SUB - 1)) + Q_SUB * st
            s = jnp.where(kpos <= qpos, s, neg_inf)
            m, l, acc = update(carry, s, vt_ref[0, hh, qi][:, :nk])
            o = acc * (1.0 / l)
            od = o[:, :Q_SUB] - lam * o[:, Q_SUB:]
            on = od * lax.rsqrt(jnp.mean(od * od, axis=0, keepdims=True) + RMS_EPS)
            rs = slice(st * Q_SUB, (st + 1) * Q_SUB)
            out_ref[0, rs, hs] = (yret_ref[0, rs, hs] + on.T * gdiff_ref[0, rs, hs]).astype(BF16)


def _diff_sample_kernel(pt_ref, lq1_ref, lk1_ref, lq2_ref, lk2_ref, dq_ref, dk_ref, dv_ref, gb_ref,
                        yret_ref, ndh_ref, *refs, n_steps):
    del pt_ref
    k_refs = refs[:PAGES_PER_STEP]
    v_refs = refs[PAGES_PER_STEP:2 * PAGES_PER_STEP]
    out_ref, qbd_ref, m_ref, l_ref, acc_ref = refs[2 * PAGES_PER_STEP:]
    g = pl.program_id(1)
    n_rows = 2 * N_DIFF_HEADS
    r = lax.broadcasted_iota(jnp.int32, (n_rows, D_MODEL), 0)
    c = lax.broadcasted_iota(jnp.int32, (n_rows, D_MODEL), 1)

    @pl.when(g == 0)
    def _():
        q = dq_ref[0] * (DIFF_QK_DIM ** -0.5)
        qf = jnp.where((c // DIFF_QK_DIM) == r, jnp.broadcast_to(q, (n_rows, D_MODEL)), 0.0)
        qbd_ref[...] = qf.astype(BF16)
        m_ref[...] = jnp.sum(qf * dk_ref[0], axis=1, keepdims=True)
        l_ref[...] = jnp.ones_like(l_ref)
        acc_ref[...] = jnp.broadcast_to(dv_ref[0], (n_rows, D_MODEL))

    qbd = qbd_ref[...]
    s = jnp.concatenate(
        [lax.dot_general(qbd, k_refs[t][0, 0].astype(BF16), (((1,), (1,)), ((), ())),
                         preferred_element_type=F32) for t in range(PAGES_PER_STEP)], axis=1)
    m_old = m_ref[...]
    m_new = jnp.maximum(m_old, jnp.max(s, axis=1, keepdims=True))
    alpha = jnp.exp(m_old - m_new)
    p = jnp.exp(s - m_new).astype(BF16)
    l_ref[...] = alpha * l_ref[...] + jnp.sum(p.astype(F32), axis=1, keepdims=True)
    pv = jnp.dot(p[:, :PAGE_SIZE], v_refs[0][0, 0].astype(BF16), preferred_element_type=F32)
    for t in range(1, PAGES_PER_STEP):
        pv = pv + jnp.dot(p[:, t * PAGE_SIZE:(t + 1) * PAGE_SIZE], v_refs[t][0, 0].astype(BF16),
                          preferred_element_type=F32)
    acc_ref[...] = acc_ref[...] * alpha + pv
    m_ref[...] = m_new

    @pl.when(g == n_steps - 1)
    def _():
        lam = _lambda(lq1_ref, lk1_ref, lq2_ref, lk2_ref)
        o = acc_ref[...] * (1.0 / l_ref[...])
        head2 = 2 * (c // DIFF_V_DIM)
        w = jnp.where(r == head2, 1.0, 0.0) - lam * jnp.where(r == head2 + 1, 1.0, 0.0)
        od = jnp.sum(w * o, axis=0, keepdims=True)
        parts = [_rms(od[:, hh * DIFF_V_DIM:(hh + 1) * DIFF_V_DIM]) for hh in range(N_DIFF_HEADS)]
        y = jnp.concatenate(parts, axis=1) * _sigmoid(gb_ref[0]) * ndh_ref[...]
        out_ref[0] = (yret_ref[0] + y).astype(BF16)


def _out_ffn_kernel(mg_ref, x_ref, wo_ref, gpost_ref, gpre_ref, gffn_ref, wg_ref, wu_ref, wd_ref, y_ref):
    mix = jnp.dot(mg_ref[...], wo_ref[...], preferred_element_type=F32)
    x1 = x_ref[...] + _rms(mix) * gpost_ref[...]
    h2 = (_rms(x1) * gpre_ref[...]).astype(BF16)
    a = jnp.dot(h2, wg_ref[...], preferred_element_type=F32)
    u = jnp.dot(h2, wu_ref[...], preferred_element_type=F32)
    act = (a * _sigmoid(a) * u).astype(BF16)
    ff = jnp.dot(act, wd_ref[...], preferred_element_type=F32)
    y_ref[...] = x1 + _rms(ff) * gffn_ref[...]


def _const_spec(shape):
    nd = len(shape)
    return pl.BlockSpec(shape, lambda *_: (0,) * nd, pipeline_mode=pl.Buffered(1))


def _params(sem):
    return pltpu.CompilerParams(dimension_semantics=sem, vmem_limit_bytes=VMEM_LIMIT)


def _rope_tables(pos):
    half = RET_QK_DIM // 2
    freqs = ROPE_BASE ** (-jnp.arange(half, dtype=F32) / half)
    ang = pos[:, None] * freqs[None, :]
    cos, sin = jnp.cos(ang), jnp.sin(ang)
    return jnp.concatenate([cos, cos], axis=1), jnp.concatenate([-sin, sin], axis=1)


def _log_gamma():
    return jnp.log(1.0 - 2.0 ** (-5.0 - jnp.arange(N_RET_HEADS, dtype=F32)))


def _out_ffn(merged, x, w_out, g_post, g_pre, g_ffn, w_gate, w_up, w_down, tm):
    n, d = x.shape
    d_ff = w_gate.shape[1]
    row = lambda i: (i, 0)
    return pl.pallas_call(
        _out_ffn_kernel,
        grid=(n // tm,),
        in_specs=[pl.BlockSpec((tm, d), row), pl.BlockSpec((tm, d), row), _const_spec((d, d)),
                  _const_spec((1, d)), _const_spec((1, d)), _const_spec((1, d)),
                  _const_spec((d, d_ff)), _const_spec((d, d_ff)), _const_spec((d_ff, d))],
        out_specs=pl.BlockSpec((tm, d), row),
        out_shape=jax.ShapeDtypeStruct((n, d), F32),
        compiler_params=_params(("parallel",)),
        name="out_ffn",
    )(merged, x, w_out, g_post, g_pre, g_ffn, w_gate, w_up, w_down)


def kernel(x_prompt, x_sample, cache_k, cache_v, state_ret, page_table, meta_tokens, norm_mix_pre,
           norm_mix_post, w_in, lambda_q1, lambda_k1, lambda_q2, lambda_k2, norm_diff_head, w_out,
           norm_ffn_pre, norm_ffn_post, w_gate, w_up, w_down):
    bsz, seq, d = x_prompt.shape
    dec_b, t_s, _ = x_sample.shape
    depth, n_pool = cache_k.shape[0], cache_k.shape[1]
    n_pages = page_table.shape[1]
    assert depth == 1 and t_s == 1 and d == D_MODEL and w_in.shape[2] == D_IN_PROJ
    assert seq % Q_TILE == 0 and n_pages % PAGES_PER_STEP == 0

    w_in_b = w_in[0].astype(BF16)
    w_out_b = w_out[0].astype(BF16)
    w_gate_b, w_up_b, w_down_b = w_gate[0].astype(BF16), w_up[0].astype(BF16), w_down[0].astype(BF16)
    g_pre = norm_mix_pre.astype(F32).reshape(1, d)
    g_post = norm_mix_post.astype(F32).reshape(1, d)
    g_fpre = norm_ffn_pre.astype(F32).reshape(1, d)
    g_fpost = norm_ffn_post.astype(F32).reshape(1, d)
    ndh = (jnp.tile(norm_diff_head.astype(F32).reshape(1, DIFF_V_DIM), (1, N_DIFF_HEADS)) * (1.0 - LAM_INIT))
    lams = [a.astype(F32).reshape(1, DIFF_QK_DIM) for a in (lambda_q1, lambda_k1, lambda_q2, lambda_k2)]
    lam_specs = [_const_spec((1, DIFF_QK_DIM))] * 4

    lg = _log_gamma()
    cos_p, sin_p = _rope_tables(jnp.arange(seq + N_META, dtype=F32))
    cos_m = jnp.pad(cos_p[:N_META], ((0, META_PAD - N_META), (0, 0)))
    sin_m = jnp.pad(sin_p[:N_META], ((0, META_PAD - N_META), (0, 0)))
    cos_r, sin_r = cos_p[N_META:], sin_p[N_META:]
    cos_s, sin_s = _rope_tables(PAST_LEN + jnp.arange(1, dtype=F32))

    ci = jnp.arange(RET_CHUNK, dtype=F32)
    diff = ci[:, None] - ci[None, :]
    decay = jnp.where(diff >= 0, jnp.exp(jnp.maximum(diff, 0.0)[None] * lg[:, None, None]), 0.0)
    qdec = jnp.broadcast_to(jnp.exp((ci + 1.0)[None, :] * lg[:, None])[:, :, None],
                            (N_RET_HEADS, RET_CHUNK, RET_V_DIM))
    kdec = jnp.broadcast_to(jnp.exp((RET_CHUNK - 1.0 - ci)[None, :] * lg[:, None])[:, :, None],
                            (N_RET_HEADS, RET_CHUNK, RET_QK_DIM))
    gchunk = jnp.broadcast_to(jnp.exp(RET_CHUNK * lg)[:, None, None], (N_RET_HEADS, 1, RET_V_DIM))
    mi = jnp.arange(META_PAD, dtype=F32)
    kdec_m = jnp.where(mi[None, :] < N_META, jnp.exp((N_META - 1.0 - mi)[None, :] * lg[:, None]), 0.0)
    kdec_m = jnp.broadcast_to(kdec_m[:, :, None], (N_RET_HEADS, META_PAD, RET_QK_DIM))
    gam1 = jnp.broadcast_to(jnp.exp(lg)[:, None, None], (N_RET_HEADS, 1, RET_V_DIM))

    mt = jnp.pad(meta_tokens.astype(F32), ((0, META_PAD - N_META), (0, 0)))
    kf_m, vf_m, kb_m, vt_m, s_meta = pl.pallas_call(
        _meta_kernel,
        grid=(1,),
        in_specs=[_const_spec((META_PAD, d)), _const_spec((1, d)),
                  pl.BlockSpec((d, 512), lambda i: (0, C_RK // 512)),
                  pl.BlockSpec((d, 1024), lambda i: (0, C_RV // 1024)),
                  pl.BlockSpec((d, 1024), lambda i: (0, C_DK // 1024)),
                  pl.BlockSpec((d, 1024), lambda i: (0, C_DV // 1024)),
                  _const_spec((META_PAD, 128)), _const_spec((META_PAD, 128)),
                  _const_spec((N_RET_HEADS, META_PAD, RET_QK_DIM))],
        out_specs=[pl.BlockSpec((META_PAD, d), lambda i: (0, 0)), pl.BlockSpec((META_PAD, d), lambda i: (0, 0)),
                   pl.BlockSpec((META_PAD, d), lambda i: (0, 0)),
                   pl.BlockSpec((N_DIFF_HEADS, 128, META_PAD), lambda i: (0, 0, 0)),
                   pl.BlockSpec((N_RET_HEADS, RET_QK_DIM, RET_V_DIM), lambda i: (0, 0, 0))],
        out_shape=[jax.ShapeDtypeStruct((META_PAD, d), F32), jax.ShapeDtypeStruct((META_PAD, d), F32),
                   jax.ShapeDtypeStruct((META_PAD, d), BF16),
                   jax.ShapeDtypeStruct((N_DIFF_HEADS, 128, META_PAD), BF16),
                   jax.ShapeDtypeStruct((N_RET_HEADS, RET_QK_DIM, RET_V_DIM), F32)],
        compiler_params=_params(("arbitrary",)),
        name="meta_proj",
    )(mt, g_pre, w_in_b, w_in_b, w_in_b, w_in_b, cos_m, sin_m, kdec_m)

    tm = 256
    n_t = seq // tm
    tile3 = lambda b, i: (b, i, 0)
    rq, rk, rv, gret, qt, kb, vt, kf, vf, gdiff = pl.pallas_call(
        functools.partial(_inproj_prompt_kernel, tm=tm),
        grid=(bsz, n_t),
        in_specs=[pl.BlockSpec((1, tm, d), tile3), _const_spec((1, d)), _const_spec((d, D_IN_PROJ)),
                  pl.BlockSpec((tm, 128), lambda b, i: (i, 0)), pl.BlockSpec((tm, 128), lambda b, i: (i, 0)),
                  _const_spec((1, d))],
        out_specs=[pl.BlockSpec((1, tm, 512), tile3), pl.BlockSpec((1, tm, 512), tile3),
                   pl.BlockSpec((1, tm, d), tile3), pl.BlockSpec((1, tm, d), tile3),
                   pl.BlockSpec((1, N_DIFF_HEADS, tm // Q_SUB, 128, Q_SUB), lambda b, i: (b, 0, i, 0, 0)),
                   pl.BlockSpec((1, tm, d), tile3),
                   pl.BlockSpec((1, N_DIFF_HEADS, tm // KV_BLOCK, 128, KV_BLOCK), lambda b, i: (b, 0, i, 0, 0)),
                   pl.BlockSpec((1, tm, d), tile3), pl.BlockSpec((1, tm, d), tile3),
                   pl.BlockSpec((1, tm, d), tile3)],
        out_shape=[jax.ShapeDtypeStruct((bsz, seq, 512), BF16), jax.ShapeDtypeStruct((bsz, seq, 512), BF16),
                   jax.ShapeDtypeStruct((bsz, seq, d), BF16), jax.ShapeDtypeStruct((bsz, seq, d), F32),
                   jax.ShapeDtypeStruct((bsz, N_DIFF_HEADS, seq // Q_SUB, 128, Q_SUB), BF16),
                   jax.ShapeDtypeStruct((bsz, seq, d), BF16),
                   jax.ShapeDtypeStruct((bsz, N_DIFF_HEADS, seq // KV_BLOCK, 128, KV_BLOCK), BF16),
                   jax.ShapeDtypeStruct((bsz, seq, d), F32), jax.ShapeDtypeStruct((bsz, seq, d), F32),
                   jax.ShapeDtypeStruct((bsz, seq, d), F32)],
        compiler_params=_params(("parallel", "parallel")),
        name="inproj_prompt",
    )(x_prompt, g_pre, w_in_b, cos_r, sin_r, ndh)

    n_chunks = seq // RET_CHUNK
    yret, ret_prompt = pl.pallas_call(
        functools.partial(_ret_prompt_kernel, n_chunks=n_chunks),
        grid=(bsz, n_chunks),
        in_specs=[pl.BlockSpec((1, RET_CHUNK, 512), tile3), pl.BlockSpec((1, RET_CHUNK, 512), tile3),
                  pl.BlockSpec((1, RET_CHUNK, d), tile3), pl.BlockSpec((1, RET_CHUNK, d), tile3),
                  _const_spec((N_RET_HEADS, RET_QK_DIM, RET_V_DIM)),
                  _const_spec((N_RET_HEADS, RET_CHUNK, RET_CHUNK)),
                  _const_spec((N_RET_HEADS, RET_CHUNK, RET_V_DIM)),
                  _const_spec((N_RET_HEADS, RET_CHUNK, RET_QK_DIM)),
                  _const_spec((N_RET_HEADS, 1, RET_V_DIM))],
        out_specs=[pl.BlockSpec((1, RET_CHUNK, d), tile3),
                   pl.BlockSpec((1, 1, N_RET_HEADS, RET_QK_DIM, RET_V_DIM), lambda b, c: (0, b, 0, 0, 0))],
        out_shape=[jax.ShapeDtypeStruct((bsz, seq, d), F32),
                   jax.ShapeDtypeStruct((1, bsz, N_RET_HEADS, RET_QK_DIM, RET_V_DIM), F32)],
        scratch_shapes=[pltpu.VMEM((N_RET_HEADS, RET_QK_DIM, RET_V_DIM), F32)],
        compiler_params=_params(("parallel", "arbitrary")),
        name="ret_prompt",
    )(rq, rk, rv, gret, s_meta, decay, qdec, kdec, gchunk)

    n_q = seq // Q_TILE
    pair = N_DIFF_HEADS // 2
    merged = pl.pallas_call(
        _diff_prompt_kernel,
        grid=(bsz, pair, n_q),
        in_specs=lam_specs + [
            pl.BlockSpec((1, 2, Q_TILE // Q_SUB, 128, Q_SUB), lambda b, h, i: (b, h, i, 0, 0)),
            pl.BlockSpec((1, seq, 256), lambda b, h, i: (b, 0, h)),
            pl.BlockSpec((1, 2, seq // KV_BLOCK, 128, KV_BLOCK), lambda b, h, i: (b, h, 0, 0, 0)),
            pl.BlockSpec((META_PAD, 256), lambda b, h, i: (0, h)),
            pl.BlockSpec((2, 128, META_PAD), lambda b, h, i: (h, 0, 0)),
            pl.BlockSpec((1, Q_TILE, 256), lambda b, h, i: (b, i, h)),
            pl.BlockSpec((1, Q_TILE, 256), lambda b, h, i: (b, i, h))],
        out_specs=pl.BlockSpec((1, Q_TILE, 256), lambda b, h, i: (b, i, h)),
        out_shape=jax.ShapeDtypeStruct((bsz, seq, d), BF16),
        compiler_params=_params(("parallel", "parallel", "arbitrary")),
        name="diff_prompt",
    )(*lams, qt, kb, vt, kb_m, vt_m, gdiff, yret)

    y_prompt = _out_ffn(merged.reshape(bsz * seq, d), x_prompt.reshape(bsz * seq, d), w_out_b, g_post,
                        g_fpre, g_fpost, w_gate_b, w_up_b, w_down_b, tm=512).reshape(bsz, seq, d)

    xs = x_sample.reshape(dec_b, d)
    proj_s = pl.pallas_call(
        _inproj_sample_kernel,
        grid=(D_IN_PROJ // 1024,),
        in_specs=[_const_spec((dec_b, d)), _const_spec((1, d)), pl.BlockSpec((d, 1024), lambda j: (0, j)),
                  _const_spec((1, 128)), _const_spec((1, 128))],
        out_specs=pl.BlockSpec((dec_b, 1024), lambda j: (0, j)),
        out_shape=jax.ShapeDtypeStruct((dec_b, D_IN_PROJ), F32),
        compiler_params=_params(("parallel",)),
        name="inproj_sample",
    )(xs, g_pre, w_in_b, cos_s, sin_s)

    rows = 8
    st_spec = pl.BlockSpec((1, rows, N_RET_HEADS, RET_QK_DIM, RET_V_DIM), lambda g: (0, g, 0, 0, 0))
    colblk = lambda j: pl.BlockSpec((rows, 1024), lambda g: (g, j))
    yret_s, ret_sample = pl.pallas_call(
        functools.partial(_ret_sample_kernel, rows=rows),
        grid=(dec_b // rows,),
        in_specs=[colblk(0), colblk(C_RV // 1024), colblk(C_RG // 1024), colblk(C_GA // 1024), st_spec,
                  _const_spec((N_RET_HEADS, 1, RET_V_DIM))],
        out_specs=[pl.BlockSpec((rows, d), lambda g: (g, 0)), st_spec],
        out_shape=[jax.ShapeDtypeStruct((dec_b, d), F32),
                   jax.ShapeDtypeStruct((1, dec_b, N_RET_HEADS, RET_QK_DIM, RET_V_DIM), F32)],
        compiler_params=_params(("parallel",)),
        name="ret_sample",
    )(proj_s, proj_s, proj_s, proj_s, state_ret.astype(F32), gam1)

    n_steps = n_pages // PAGES_PER_STEP
    proj_s3 = proj_s.reshape(dec_b, 1, D_IN_PROJ)
    ck = cache_k.reshape(depth, n_pool, PAGE_SIZE, d)
    cv = cache_v.reshape(depth, n_pool, PAGE_SIZE, d)
    row3 = lambda j: pl.BlockSpec((1, 1, 1024), lambda b, g, pt: (b, 0, j))
    cst = lambda shape: pl.BlockSpec(shape, lambda b, g, pt: (0,) * len(shape))

    def page_spec(t):
        return pl.BlockSpec((1, 1, PAGE_SIZE, d), lambda b, g, pt: (0, pt[b, g * PAGES_PER_STEP + t], 0, 0))

    page_specs = [page_spec(t) for t in range(PAGES_PER_STEP)]
    n_rows = 2 * N_DIFF_HEADS
    merged_s = pl.pallas_call(
        functools.partial(_diff_sample_kernel, n_steps=n_steps),
        grid_spec=pltpu.PrefetchScalarGridSpec(
            num_scalar_prefetch=1,
            grid=(dec_b, n_steps),
            in_specs=[cst((1, DIFF_QK_DIM))] * 4 + [row3(C_DQ // 1024), row3(C_DK // 1024), row3(C_DV // 1024),
                                                    row3(C_GB // 1024), row3(0), cst((1, d))]
                     + page_specs + page_specs,
            out_specs=pl.BlockSpec((1, 1, d), lambda b, g, pt: (b, 0, 0)),
            scratch_shapes=[pltpu.VMEM((n_rows, d), BF16), pltpu.VMEM((n_rows, 1), F32),
                            pltpu.VMEM((n_rows, 1), F32), pltpu.VMEM((n_rows, d), F32)]),
        out_shape=jax.ShapeDtypeStruct((dec_b, 1, d), BF16),
        compiler_params=_params(("parallel", "arbitrary")),
        name="diff_sample",
    )(page_table.astype(jnp.int32), *lams, proj_s3, proj_s3, proj_s3, proj_s3, yret_s.reshape(dec_b, 1, d), ndh,
      *([ck] * PAGES_PER_STEP), *([cv] * PAGES_PER_STEP))

    y_sample = _out_ffn(merged_s.reshape(dec_b, d), xs, w_out_b, g_post, g_fpre, g_fpost, w_gate_b, w_up_b,
                        w_down_b, tm=dec_b).reshape(dec_b, 1, d)

    def with_meta(meta_rows, real):
        meta_b = jnp.broadcast_to(meta_rows[None, :N_META], (bsz, N_META, d))
        full = jnp.concatenate([meta_b, real], axis=1)
        return full.reshape(1, bsz, seq + N_META, N_DIFF_HEADS, DIFF_V_DIM)

    k_prompt = with_meta(kf_m, kf)
    v_prompt = with_meta(vf_m, vf)
    k_sample = proj_s[:, C_DK:C_DK + 1024].reshape(1, dec_b, 1, N_DIFF_HEADS, 2 * DIFF_QK_DIM)
    v_sample = proj_s[:, C_DV:C_DV + 1024].reshape(1, dec_b, 1, N_DIFF_HEADS, DIFF_V_DIM)
    return (y_prompt, y_sample, k_prompt, v_prompt, ret_prompt, k_sample, v_sample, ret_sample)
```

```python
import functools

import jax
import jax.numpy as jnp
from jax import lax
from jax.experimental import pallas as pl
from jax.experimental.pallas import tpu as pltpu

F32 = jnp.float32
BF16 = jnp.bfloat16

D_MODEL = 1024
N_META = 16
N_RET_HEADS = 4
RET_QK_DIM = 128
RET_V_DIM = 256
N_DIFF_HEADS = 8
DIFF_QK_DIM = 64
DIFF_V_DIM = 128
PAGE_SIZE = 128
PAST_LEN = 8192
ROPE_BASE = 10000.0
RMS_EPS = 1e-6
LAM_INIT = 0.8 - 0.6 * 1.0
LOG2_E = 1.4426950408889634

C_RQ, C_RK, C_RV, C_RG, C_DQ, C_DK, C_DV, C_GA, C_GB = 0, 512, 1024, 2048, 3072, 4096, 5120, 6144, 7168
D_IN_PROJ = 8192

META_PAD = 128
RET_CHUNK = 256
Q_TILE = 256
Q_SUB = 128
KV_BLOCK = 256
HEADS_PER_STEP = 8
PAGES_PER_STEP = 8
VMEM_LIMIT = 56 * 1024 * 1024


def _rms(x):
    return x * lax.rsqrt(jnp.mean(x * x, axis=-1, keepdims=True) + RMS_EPS)


def _sigmoid(x):
    return 1.0 / (1.0 + jnp.exp(-x))


def _rope(p, cos, sin, n_heads):
    parts = []
    for hh in range(n_heads):
        r = p[:, hh * 128:(hh + 1) * 128]
        parts.append(r * cos + pltpu.roll(r, 64, 1) * sin)
    return parts


def _store_per_head(ref, x):
    for hh in range(N_DIFF_HEADS):
        ref[:, hh, :] = x[:, hh * 128:(hh + 1) * 128]


def _lambda(lq1_ref, lk1_ref, lq2_ref, lk2_ref):
    a = jnp.sum(lq1_ref[...] * lk1_ref[...], axis=-1, keepdims=True)
    b = jnp.sum(lq2_ref[...] * lk2_ref[...], axis=-1, keepdims=True)
    return jnp.exp(a) - jnp.exp(b) + LAM_INIT


def _inproj_prompt_kernel(x_ref, g_ref, w_ref, cos_ref, sin_ref, ndh_ref,
                          rq_ref, rk_ref, rv_ref, gret_ref, qt_ref, kb_ref, vt_ref, kf_ref, vf_ref,
                          gdiff_ref, *, tm):
    h = (_rms(x_ref[0]) * g_ref[...]).astype(BF16)

    def proj(c0, n):
        return jnp.dot(h, w_ref[:, c0:c0 + n], preferred_element_type=F32)

    cos = cos_ref[...]
    sin = sin_ref[...]
    rq_ref[0] = jnp.concatenate(_rope(proj(C_RQ, 512), cos, sin, N_RET_HEADS), axis=1).astype(BF16)
    rk = jnp.concatenate(_rope(proj(C_RK, 512), cos, sin, N_RET_HEADS), axis=1)
    rk_ref[0] = (rk * (RET_QK_DIM ** -0.5)).astype(BF16)
    rv_ref[0] = proj(C_RV, 1024).astype(BF16)
    rg = proj(C_RG, 1024)
    ga = proj(C_GA, 1024)
    gret_ref[0] = rg * _sigmoid(rg) * _sigmoid(ga)
    dq = proj(C_DQ, 1024) * (DIFF_QK_DIM ** -0.5 * LOG2_E)
    for hh in range(N_DIFF_HEADS):
        for s in range(tm // Q_SUB):
            qt_ref[0, hh, s] = dq[s * Q_SUB:(s + 1) * Q_SUB, hh * 128:(hh + 1) * 128].T.astype(BF16)
    dk = proj(C_DK, 1024)
    _store_per_head(kf_ref, dk)
    kb_ref[0] = dk.astype(BF16)
    dv = proj(C_DV, 1024)
    _store_per_head(vf_ref, dv)
    for hh in range(N_DIFF_HEADS):
        for s in range(tm // KV_BLOCK):
            vt_ref[0, hh, s] = dv[s * KV_BLOCK:(s + 1) * KV_BLOCK, hh * 128:(hh + 1) * 128].T.astype(BF16)
    gb = proj(C_GB, 1024)
    gdiff_ref[0] = _sigmoid(gb) * ndh_ref[...]


def _meta_kernel(mt_ref, g_ref, wrk_ref, wrv_ref, wdk_ref, wdv_ref, cos_ref, sin_ref, kdec_ref,
                 kf_ref, vf_ref, kb_ref, vt_ref, s_ref):
    h = (_rms(mt_ref[...]) * g_ref[...]).astype(BF16)
    rk = _rope(jnp.dot(h, wrk_ref[...], preferred_element_type=F32), cos_ref[...], sin_ref[...], N_RET_HEADS)
    rv = jnp.dot(h, wrv_ref[...], preferred_element_type=F32).astype(BF16)
    dk = jnp.dot(h, wdk_ref[...], preferred_element_type=F32)
    dv = jnp.dot(h, wdv_ref[...], preferred_element_type=F32)
    _store_per_head(kf_ref, dk[0:N_META])
    _store_per_head(vf_ref, dv[0:N_META])
    kb_ref[...] = dk.astype(BF16)
    for hh in range(N_DIFF_HEADS):
        vt_ref[hh] = dv[:, hh * 128:(hh + 1) * 128].T.astype(BF16)
    for hh in range(N_RET_HEADS):
        kd = (rk[hh] * (RET_QK_DIM ** -0.5)) * kdec_ref[hh]
        s_ref[hh] = jnp.dot(kd.T.astype(BF16), rv[:, hh * RET_V_DIM:(hh + 1) * RET_V_DIM],
                            preferred_element_type=F32)


def _meta_rows_kernel(k_hbm, v_hbm, km_ref, vm_ref, k_ref, v_ref):
    del k_hbm, v_hbm
    k_ref[...] = km_ref[...]
    v_ref[...] = vm_ref[...]


def _inproj_sample_kernel(x_ref, g_ref, w_ref, cos_ref, sin_ref, out_ref):
    j = pl.program_id(0)
    h = (_rms(x_ref[...]) * g_ref[...]).astype(BF16)
    p = jnp.dot(h, w_ref[...], preferred_element_type=F32)

    @pl.when(j == 0)
    def _():
        parts = _rope(p, cos_ref[...], sin_ref[...], 2 * N_RET_HEADS)
        parts = parts[:N_RET_HEADS] + [r * (RET_QK_DIM ** -0.5) for r in parts[N_RET_HEADS:]]
        out_ref[...] = jnp.concatenate(parts, axis=1)

    @pl.when(j != 0)
    def _():
        out_ref[...] = p


def _ret_prompt_kernel(rq_ref, rk_ref, rv_ref, gret_ref, smeta_ref, decay_ref, qdec_ref, kdec_ref, gc_ref,
                       y_ref, sout_ref, state_ref, *, n_chunks):
    c = pl.program_id(1)

    @pl.when(c == 0)
    def _():
        state_ref[...] = smeta_ref[...]

    for hh in range(N_RET_HEADS):
        q = rq_ref[0, :, hh * 128:(hh + 1) * 128]
        k = rk_ref[0, :, hh * 128:(hh + 1) * 128]
        v = rv_ref[0, :, hh * RET_V_DIM:(hh + 1) * RET_V_DIM]
        s_old = state_ref[hh]
        sc = lax.dot_general(q, k, (((1,), (1,)), ((), ())), preferred_element_type=F32) * decay_ref[hh]
        o = jnp.dot(sc.astype(BF16), v, preferred_element_type=F32)
        o = o + jnp.dot(q, s_old.astype(BF16), preferred_element_type=F32) * qdec_ref[hh]
        kd = k.astype(F32) * kdec_ref[hh]
        state_ref[hh] = s_old * gc_ref[hh] + jnp.dot(kd.T.astype(BF16), v, preferred_element_type=F32)
        y_ref[0, :, hh * RET_V_DIM:(hh + 1) * RET_V_DIM] = (
            _rms(o) * gret_ref[0, :, hh * RET_V_DIM:(hh + 1) * RET_V_DIM])

    @pl.when(c == n_chunks - 1)
    def _():
        sout_ref[0, 0] = state_ref[...]


def _ret_sample_kernel(rqk_ref, rv_ref, rg_ref, ga_ref, st_ref, gam_ref, y_ref, snew_ref, *, rows):
    eye = (lax.broadcasted_iota(jnp.int32, (128, 128), 0) == lax.broadcasted_iota(jnp.int32, (128, 128), 1))

    def column(r):
        return jnp.sum(jnp.where(eye, jnp.broadcast_to(r, (128, 128)), 0.0), axis=1, keepdims=True)

    for bl in range(rows):
        for hh in range(N_RET_HEADS):
            q = rqk_ref[bl:bl + 1, hh * 128:(hh + 1) * 128]
            k = rqk_ref[bl:bl + 1, 512 + hh * 128:512 + (hh + 1) * 128]
            cs = slice(hh * RET_V_DIM, (hh + 1) * RET_V_DIM)
            v = rv_ref[bl:bl + 1, cs]
            s_old = st_ref[0, bl, hh]
            gam = gam_ref[hh]
            score = jnp.sum(q * k, axis=1, keepdims=True)
            o = score * v + jnp.sum(column(q) * s_old, axis=0, keepdims=True) * gam
            snew_ref[0, bl, hh] = s_old * gam + column(k) * v
            rg = rg_ref[bl:bl + 1, cs]
            y_ref[bl:bl + 1, cs] = _rms(o) * (rg * _sigmoid(rg)) * _sigmoid(ga_ref[bl:bl + 1, cs])


def _diff_prompt_kernel(lq1_ref, lk1_ref, lq2_ref, lk2_ref, qt_ref, k_ref, vt_ref, km_ref, vtm_ref,
                        gdiff_ref, yret_ref, out_ref, qbd_scr, m_scr, l_scr, acc_scr):
    qi = pl.program_id(2)
    wide = 2 * Q_SUB
    row = lax.broadcasted_iota(jnp.int32, (128, wide), 0)
    col = lax.broadcasted_iota(jnp.int32, (128, wide), 1)
    bd_mask = (row < DIFF_QK_DIM) == (col < Q_SUB)
    n_sub = Q_TILE // Q_SUB
    units = [(hh, st) for hh in range(HEADS_PER_STEP) for st in range(n_sub)]

    def scores(keys):
        return [jnp.dot(kb, qbd_scr[u], preferred_element_type=F32) for u, kb in enumerate(keys)]

    def softmax_pv(ss, vts):
        probs = []
        for u, s in enumerate(ss):
            m_old = m_scr[u]
            m_new = jnp.maximum(m_old, jnp.max(s, axis=0, keepdims=True))
            alpha = jnp.exp2(m_old - m_new)
            p = jnp.exp2(s - m_new)
            l_scr[u] = alpha * l_scr[u] + jnp.sum(p, axis=0, keepdims=True)
            m_scr[u] = m_new
            probs.append((alpha, p.astype(BF16)))
        for u, (vt, (alpha, p)) in enumerate(zip(vts, probs)):
            acc_scr[u] = acc_scr[u] * alpha + jnp.dot(vt, p, preferred_element_type=F32)

    def key_block(j):
        off = pl.multiple_of(j * KV_BLOCK, KV_BLOCK)
        return [k_ref[0, pl.ds(off, KV_BLOCK), hh * 128:(hh + 1) * 128] for hh, _ in units]

    for u, (hh, st) in enumerate(units):
        qt = qt_ref[0, hh, st]
        qbd_scr[u] = jnp.where(bd_mask, jnp.concatenate([qt, qt], axis=1), jnp.zeros((), BF16))
    m_scr[...] = jnp.full(m_scr.shape, -jnp.inf, F32)
    l_scr[...] = jnp.zeros(l_scr.shape, F32)
    acc_scr[...] = jnp.zeros(acc_scr.shape, F32)

    softmax_pv(scores([km_ref[0:N_META, hh * 128:(hh + 1) * 128] for hh, _ in units]),
               [vtm_ref[hh][:, 0:N_META] for hh, _ in units])

    def body(j, carry):
        softmax_pv(scores(key_block(j)), [vt_ref[0, hh, j] for hh, _ in units])
        return carry

    lax.fori_loop(0, qi, body, 0)

    def visible(nk):
        kpos = lax.broadcasted_iota(jnp.int32, (nk, wide), 0)
        qpos = (lax.broadcasted_iota(jnp.int32, (nk, wide), 1) & (Q_SUB - 1)) + (nk - Q_SUB)
        return kpos <= qpos

    off = pl.multiple_of(qi * KV_BLOCK, KV_BLOCK)
    ends = [Q_SUB * (st + 1) for _, st in units]
    diag = scores([k_ref[0, pl.ds(off, nk), hh * 128:(hh + 1) * 128] for (hh, _), nk in zip(units, ends)])
    softmax_pv([jnp.where(visible(nk), s, -jnp.inf) for s, nk in zip(diag, ends)],
               [vt_ref[0, hh, qi][:, 0:nk] for (hh, _), nk in zip(units, ends)])

    lam = _lambda(lq1_ref, lk1_ref, lq2_ref, lk2_ref)
    for u, (hh, st) in enumerate(units):
        hs = slice(hh * 128, (hh + 1) * 128)
        o = acc_scr[u] * (1.0 / l_scr[u])
        od = o[:, :Q_SUB] - lam * o[:, Q_SUB:]
        on = od * lax.rsqrt(jnp.mean(od * od, axis=0, keepdims=True) + RMS_EPS)
        rs = slice(st * Q_SUB, (st + 1) * Q_SUB)
        out_ref[0, rs, hs] = (yret_ref[0, rs, hs] + on.T * gdiff_ref[0, rs, hs]).astype(BF16)


def _diff_sample_kernel(pt_ref, lq1_ref, lk1_ref, lq2_ref, lk2_ref, q_ref, kn_ref, vn_ref, gb_ref,
                        yret_ref, ndh_ref, *refs, n_steps):
    del pt_ref
    k_refs = refs[:PAGES_PER_STEP]
    v_refs = refs[PAGES_PER_STEP:2 * PAGES_PER_STEP]
    out_ref, qall_ref, m_ref, l_ref, acc_ref = refs[2 * PAGES_PER_STEP:]
    g = pl.program_id(1)
    n_rows = 2 * N_DIFF_HEADS
    n_kh = PAGE_SIZE * N_DIFF_HEADS

    @pl.when(g == 0)
    def _():
        q = q_ref[0] * (DIFF_QK_DIM ** -0.5)
        lane = lax.broadcasted_iota(jnp.int32, q.shape, 1)
        qall = jnp.concatenate([jnp.where(lane < DIFF_QK_DIM, q, 0.0),
                                jnp.where(lane >= DIFF_QK_DIM, q, 0.0)], axis=0)
        qall_ref[...] = qall.astype(BF16)
        kn = kn_ref[0]
        vn = vn_ref[0]
        m_ref[...] = jnp.sum(qall * jnp.concatenate([kn, kn], axis=0), axis=1, keepdims=True)
        l_ref[...] = jnp.ones_like(l_ref)
        acc_ref[...] = jnp.concatenate([vn, vn], axis=0)

    qall = qall_ref[...]
    own_head = ((lax.broadcasted_iota(jnp.int32, (n_rows, n_kh), 1) & (N_DIFF_HEADS - 1))
                == (lax.broadcasted_iota(jnp.int32, (n_rows, n_kh), 0) & (N_DIFF_HEADS - 1)))
    s_pages = []
    for t in range(PAGES_PER_STEP):
        k2 = k_refs[t][0, 0].reshape(n_kh, 128).astype(BF16)
        s = lax.dot_general(qall, k2, (((1,), (1,)), ((), ())), preferred_element_type=F32)
        s_pages.append(jnp.where(own_head, s, -jnp.inf))
    s = jnp.concatenate(s_pages, axis=1)
    m_old = m_ref[...]
    m_new = jnp.maximum(m_old, jnp.max(s, axis=1, keepdims=True))
    alpha = jnp.exp(m_old - m_new)
    p = jnp.exp(s - m_new)
    l_ref[...] = alpha * l_ref[...] + jnp.sum(p, axis=1, keepdims=True)
    p = p.astype(BF16)
    pv = None
    for t in range(PAGES_PER_STEP):
        v2 = v_refs[t][0, 0].reshape(n_kh, 128).astype(BF16)
        d = jnp.dot(p[:, t * n_kh:(t + 1) * n_kh], v2, preferred_element_type=F32)
        pv = d if pv is None else pv + d
    acc_ref[...] = acc_ref[...] * alpha + pv
    m_ref[...] = m_new

    @pl.when(g == n_steps - 1)
    def _():
        lam = _lambda(lq1_ref, lk1_ref, lq2_ref, lk2_ref)
        o = acc_ref[...] * (1.0 / l_ref[...])
        od = o[0:N_DIFF_HEADS] - lam * o[N_DIFF_HEADS:]
        y = _rms(od) * _sigmoid(gb_ref[0]) * ndh_ref[...]
        out_ref[0] = (yret_ref[0] + y).astype(BF16)


def _out_ffn_kernel(mg_ref, x_ref, wo_ref, gpost_ref, gpre_ref, gffn_ref, wg_ref, wu_ref, wd_ref, y_ref):
    mix = jnp.dot(mg_ref[...], wo_ref[...], preferred_element_type=F32)
    x1 = x_ref[...] + _rms(mix) * gpost_ref[...]
    h2 = (_rms(x1) * gpre_ref[...]).astype(BF16)
    a = jnp.dot(h2, wg_ref[...], preferred_element_type=F32)
    u = jnp.dot(h2, wu_ref[...], preferred_element_type=F32)
    act = (a * _sigmoid(a) * u).astype(BF16)
    ff = jnp.dot(act, wd_ref[...], preferred_element_type=F32)
    y_ref[...] = x1 + _rms(ff) * gffn_ref[...]


def _const_spec(shape):
    nd = len(shape)
    return pl.BlockSpec(shape, lambda *_: (0,) * nd, pipeline_mode=pl.Buffered(1))


def _params(sem):
    return pltpu.CompilerParams(dimension_semantics=sem, vmem_limit_bytes=VMEM_LIMIT)


def _rope_tables(pos):
    half = RET_QK_DIM // 2
    freqs = ROPE_BASE ** (-jnp.arange(half, dtype=F32) / half)
    ang = pos[:, None] * freqs[None, :]
    cos, sin = jnp.cos(ang), jnp.sin(ang)
    return jnp.concatenate([cos, cos], axis=1), jnp.concatenate([-sin, sin], axis=1)


def _log_gamma():
    return jnp.log(1.0 - 2.0 ** (-5.0 - jnp.arange(N_RET_HEADS, dtype=F32)))


def _out_ffn(merged, x, w_out, g_post, g_pre, g_ffn, w_gate, w_up, w_down, tm):
    n, d = x.shape
    d_ff = w_gate.shape[1]
    row = lambda i: (i, 0)
    return pl.pallas_call(
        _out_ffn_kernel,
        grid=(n // tm,),
        in_specs=[pl.BlockSpec((tm, d), row), pl.BlockSpec((tm, d), row), _const_spec((d, d)),
                  _const_spec((1, d)), _const_spec((1, d)), _const_spec((1, d)),
                  _const_spec((d, d_ff)), _const_spec((d, d_ff)), _const_spec((d_ff, d))],
        out_specs=pl.BlockSpec((tm, d), row),
        out_shape=jax.ShapeDtypeStruct((n, d), F32),
        compiler_params=_params(("parallel",)),
        name="out_ffn",
    )(merged, x, w_out, g_post, g_pre, g_ffn, w_gate, w_up, w_down)


def kernel(x_prompt, x_sample, cache_k, cache_v, state_ret, page_table, meta_tokens, norm_mix_pre,
           norm_mix_post, w_in, lambda_q1, lambda_k1, lambda_q2, lambda_k2, norm_diff_head, w_out,
           norm_ffn_pre, norm_ffn_post, w_gate, w_up, w_down):
    bsz, seq, d = x_prompt.shape
    dec_b, t_s, _ = x_sample.shape
    depth, n_pool = cache_k.shape[0], cache_k.shape[1]
    n_pages = page_table.shape[1]
    assert depth == 1 and t_s == 1 and d == D_MODEL and w_in.shape[2] == D_IN_PROJ
    assert seq % Q_TILE == 0 and n_pages % PAGES_PER_STEP == 0

    w_in_b = w_in[0].astype(BF16)
    w_out_b = w_out[0].astype(BF16)
    w_gate_b, w_up_b, w_down_b = w_gate[0].astype(BF16), w_up[0].astype(BF16), w_down[0].astype(BF16)
    g_pre = norm_mix_pre.astype(F32).reshape(1, d)
    g_post = norm_mix_post.astype(F32).reshape(1, d)
    g_fpre = norm_ffn_pre.astype(F32).reshape(1, d)
    g_fpost = norm_ffn_post.astype(F32).reshape(1, d)
    ndh = (jnp.tile(norm_diff_head.astype(F32).reshape(1, DIFF_V_DIM), (1, N_DIFF_HEADS)) * (1.0 - LAM_INIT))
    lams = [a.astype(F32).reshape(1, DIFF_QK_DIM) for a in (lambda_q1, lambda_k1, lambda_q2, lambda_k2)]
    lam_specs = [_const_spec((1, DIFF_QK_DIM))] * 4

    lg = _log_gamma()
    cos_p, sin_p = _rope_tables(jnp.arange(seq + N_META, dtype=F32))
    cos_m = jnp.pad(cos_p[:N_META], ((0, META_PAD - N_META), (0, 0)))
    sin_m = jnp.pad(sin_p[:N_META], ((0, META_PAD - N_META), (0, 0)))
    cos_r, sin_r = cos_p[N_META:], sin_p[N_META:]
    cos_s, sin_s = _rope_tables(PAST_LEN + jnp.arange(1, dtype=F32))

    ci = jnp.arange(RET_CHUNK, dtype=F32)
    diff = ci[:, None] - ci[None, :]
    decay = jnp.where(diff >= 0, jnp.exp(jnp.maximum(diff, 0.0)[None] * lg[:, None, None]), 0.0)
    qdec = jnp.broadcast_to(jnp.exp((ci + 1.0)[None, :] * lg[:, None])[:, :, None],
                            (N_RET_HEADS, RET_CHUNK, RET_V_DIM))
    kdec = jnp.broadcast_to(jnp.exp((RET_CHUNK - 1.0 - ci)[None, :] * lg[:, None])[:, :, None],
                            (N_RET_HEADS, RET_CHUNK, RET_QK_DIM))
    gchunk = jnp.broadcast_to(jnp.exp(RET_CHUNK * lg)[:, None, None], (N_RET_HEADS, 1, RET_V_DIM))
    mi = jnp.arange(META_PAD, dtype=F32)
    kdec_m = jnp.where(mi[None, :] < N_META, jnp.exp((N_META - 1.0 - mi)[None, :] * lg[:, None]), 0.0)
    kdec_m = jnp.broadcast_to(kdec_m[:, :, None], (N_RET_HEADS, META_PAD, RET_QK_DIM))
    gam1 = jnp.broadcast_to(jnp.exp(lg)[:, None, None], (N_RET_HEADS, 1, RET_V_DIM))

    mt = jnp.pad(meta_tokens.astype(F32), ((0, META_PAD - N_META), (0, 0)))
    kf_m, vf_m, kb_m, vt_m, s_meta = pl.pallas_call(
        _meta_kernel,
        grid=(1,),
        in_specs=[_const_spec((META_PAD, d)), _const_spec((1, d)),
                  pl.BlockSpec((d, 512), lambda i: (0, C_RK // 512)),
                  pl.BlockSpec((d, 1024), lambda i: (0, C_RV // 1024)),
                  pl.BlockSpec((d, 1024), lambda i: (0, C_DK // 1024)),
                  pl.BlockSpec((d, 1024), lambda i: (0, C_DV // 1024)),
                  _const_spec((META_PAD, 128)), _const_spec((META_PAD, 128)),
                  _const_spec((N_RET_HEADS, META_PAD, RET_QK_DIM))],
        out_specs=[pl.BlockSpec((N_META, N_DIFF_HEADS, 128), lambda i: (0, 0, 0)),
                   pl.BlockSpec((N_META, N_DIFF_HEADS, 128), lambda i: (0, 0, 0)),
                   pl.BlockSpec((META_PAD, d), lambda i: (0, 0)),
                   pl.BlockSpec((N_DIFF_HEADS, 128, META_PAD), lambda i: (0, 0, 0)),
                   pl.BlockSpec((N_RET_HEADS, RET_QK_DIM, RET_V_DIM), lambda i: (0, 0, 0))],
        out_shape=[jax.ShapeDtypeStruct((N_META, N_DIFF_HEADS, 128), F32),
                   jax.ShapeDtypeStruct((N_META, N_DIFF_HEADS, 128), F32),
                   jax.ShapeDtypeStruct((META_PAD, d), BF16),
                   jax.ShapeDtypeStruct((N_DIFF_HEADS, 128, META_PAD), BF16),
                   jax.ShapeDtypeStruct((N_RET_HEADS, RET_QK_DIM, RET_V_DIM), F32)],
        compiler_params=_params(("arbitrary",)),
        name="meta_proj",
    )(mt, g_pre, w_in_b, w_in_b, w_in_b, w_in_b, cos_m, sin_m, kdec_m)

    tm = 256
    n_t = seq // tm
    tile3 = lambda b, i: (b, i, 0)
    kv_out_shape = jax.ShapeDtypeStruct((1, bsz, seq + N_META, N_DIFF_HEADS, DIFF_V_DIM), F32)
    kv_out_spec = pl.BlockSpec(
        (pl.Squeezed(), pl.Squeezed(), pl.Element(tm), pl.Element(N_DIFF_HEADS), pl.Element(DIFF_V_DIM)),
        lambda b, i: (0, b, N_META + i * tm, 0, 0))
    rq, rk, rv, gret, qt, kb, vt, kf, vf, gdiff = pl.pallas_call(
        functools.partial(_inproj_prompt_kernel, tm=tm),
        grid=(bsz, n_t),
        in_specs=[pl.BlockSpec((1, tm, d), tile3), _const_spec((1, d)), _const_spec((d, D_IN_PROJ)),
                  pl.BlockSpec((tm, 128), lambda b, i: (i, 0)), pl.BlockSpec((tm, 128), lambda b, i: (i, 0)),
                  _const_spec((1, d))],
        out_specs=[pl.BlockSpec((1, tm, 512), tile3), pl.BlockSpec((1, tm, 512), tile3),
                   pl.BlockSpec((1, tm, d), tile3), pl.BlockSpec((1, tm, d), tile3),
                   pl.BlockSpec((1, N_DIFF_HEADS, tm // Q_SUB, 128, Q_SUB), lambda b, i: (b, 0, i, 0, 0)),
                   pl.BlockSpec((1, tm, d), tile3),
                   pl.BlockSpec((1, N_DIFF_HEADS, tm // KV_BLOCK, 128, KV_BLOCK), lambda b, i: (b, 0, i, 0, 0)),
                   kv_out_spec, kv_out_spec,
                   pl.BlockSpec((1, tm, d), tile3)],
        out_shape=[jax.ShapeDtypeStruct((bsz, seq, 512), BF16), jax.ShapeDtypeStruct((bsz, seq, 512), BF16),
                   jax.ShapeDtypeStruct((bsz, seq, d), BF16), jax.ShapeDtypeStruct((bsz, seq, d), F32),
                   jax.ShapeDtypeStruct((bsz, N_DIFF_HEADS, seq // Q_SUB, 128, Q_SUB), BF16),
                   jax.ShapeDtypeStruct((bsz, seq, d), BF16),
                   jax.ShapeDtypeStruct((bsz, N_DIFF_HEADS, seq // KV_BLOCK, 128, KV_BLOCK), BF16),
                   kv_out_shape, kv_out_shape,
                   jax.ShapeDtypeStruct((bsz, seq, d), F32)],
        compiler_params=_params(("parallel", "parallel")),
        name="inproj_prompt",
    )(x_prompt, g_pre, w_in_b, cos_r, sin_r, ndh)

    n_chunks = seq // RET_CHUNK
    yret, ret_prompt = pl.pallas_call(
        functools.partial(_ret_prompt_kernel, n_chunks=n_chunks),
        grid=(bsz, n_chunks),
        in_specs=[pl.BlockSpec((1, RET_CHUNK, 512), tile3), pl.BlockSpec((1, RET_CHUNK, 512), tile3),
                  pl.BlockSpec((1, RET_CHUNK, d), tile3), pl.BlockSpec((1, RET_CHUNK, d), tile3),
                  _const_spec((N_RET_HEADS, RET_QK_DIM, RET_V_DIM)),
                  _const_spec((N_RET_HEADS, RET_CHUNK, RET_CHUNK)),
                  _const_spec((N_RET_HEADS, RET_CHUNK, RET_V_DIM)),
                  _const_spec((N_RET_HEADS, RET_CHUNK, RET_QK_DIM)),
                  _const_spec((N_RET_HEADS, 1, RET_V_DIM))],
        out_specs=[pl.BlockSpec((1, RET_CHUNK, d), tile3),
                   pl.BlockSpec((1, 1, N_RET_HEADS, RET_QK_DIM, RET_V_DIM), lambda b, c: (0, b, 0, 0, 0))],
        out_shape=[jax.ShapeDtypeStruct((bsz, seq, d), F32),
                   jax.ShapeDtypeStruct((1, bsz, N_RET_HEADS, RET_QK_DIM, RET_V_DIM), F32)],
        scratch_shapes=[pltpu.VMEM((N_RET_HEADS, RET_QK_DIM, RET_V_DIM), F32)],
        compiler_params=_params(("parallel", "arbitrary")),
        name="ret_prompt",
    )(rq, rk, rv, gret, s_meta, decay, qdec, kdec, gchunk)

    n_q = seq // Q_TILE
    hps = HEADS_PER_STEP
    n_units = hps * (Q_TILE // Q_SUB)
    merged = pl.pallas_call(
        _diff_prompt_kernel,
        grid=(bsz, N_DIFF_HEADS // hps, n_q),
        in_specs=lam_specs + [
            pl.BlockSpec((1, hps, Q_TILE // Q_SUB, 128, Q_SUB), lambda b, h, i: (b, h, i, 0, 0)),
            pl.BlockSpec((1, seq, hps * 128), lambda b, h, i: (b, 0, h)),
            pl.BlockSpec((1, hps, seq // KV_BLOCK, 128, KV_BLOCK), lambda b, h, i: (b, h, 0, 0, 0)),
            pl.BlockSpec((META_PAD, hps * 128), lambda b, h, i: (0, h)),
            pl.BlockSpec((hps, 128, META_PAD), lambda b, h, i: (h, 0, 0)),
            pl.BlockSpec((1, Q_TILE, hps * 128), lambda b, h, i: (b, i, h)),
            pl.BlockSpec((1, Q_TILE, hps * 128), lambda b, h, i: (b, i, h))],
        out_specs=pl.BlockSpec((1, Q_TILE, hps * 128), lambda b, h, i: (b, i, h)),
        out_shape=jax.ShapeDtypeStruct((bsz, seq, d), BF16),
        scratch_shapes=[pltpu.VMEM((n_units, 128, 2 * Q_SUB), BF16), pltpu.VMEM((n_units, 1, 2 * Q_SUB), F32),
                        pltpu.VMEM((n_units, 1, 2 * Q_SUB), F32),
                        pltpu.VMEM((n_units, DIFF_V_DIM, 2 * Q_SUB), F32)],
        compiler_params=_params(("parallel", "parallel", "arbitrary")),
        name="diff_prompt",
    )(*lams, qt, kb, vt, kb_m, vt_m, gdiff, yret)

    y_prompt = _out_ffn(merged.reshape(bsz * seq, d), x_prompt.reshape(bsz * seq, d), w_out_b, g_post,
                        g_fpre, g_fpost, w_gate_b, w_up_b, w_down_b, tm=512).reshape(bsz, seq, d)

    xs = x_sample.reshape(dec_b, d)
    proj_s = pl.pallas_call(
        _inproj_sample_kernel,
        grid=(D_IN_PROJ // 1024,),
        in_specs=[_const_spec((dec_b, d)), _const_spec((1, d)), pl.BlockSpec((d, 1024), lambda j: (0, j)),
                  _const_spec((1, 128)), _const_spec((1, 128))],
        out_specs=pl.BlockSpec((dec_b, 1024), lambda j: (0, j)),
        out_shape=jax.ShapeDtypeStruct((dec_b, D_IN_PROJ), F32),
        compiler_params=_params(("parallel",)),
        name="inproj_sample",
    )(xs, g_pre, w_in_b, cos_s, sin_s)

    rows = 8
    st_spec = pl.BlockSpec((1, rows, N_RET_HEADS, RET_QK_DIM, RET_V_DIM), lambda g: (0, g, 0, 0, 0))
    colblk = lambda j: pl.BlockSpec((rows, 1024), lambda g: (g, j))
    yret_s, ret_sample = pl.pallas_call(
        functools.partial(_ret_sample_kernel, rows=rows),
        grid=(dec_b // rows,),
        in_specs=[colblk(0), colblk(C_RV // 1024), colblk(C_RG // 1024), colblk(C_GA // 1024), st_spec,
                  _const_spec((N_RET_HEADS, 1, RET_V_DIM))],
        out_specs=[pl.BlockSpec((rows, d), lambda g: (g, 0)), st_spec],
        out_shape=[jax.ShapeDtypeStruct((dec_b, d), F32),
                   jax.ShapeDtypeStruct((1, dec_b, N_RET_HEADS, RET_QK_DIM, RET_V_DIM), F32)],
        compiler_params=_params(("parallel",)),
        name="ret_sample",
    )(proj_s, proj_s, proj_s, proj_s, state_ret.astype(F32), gam1)

    n_steps = n_pages // PAGES_PER_STEP
    per_head = lambda a: a.reshape(dec_b, N_DIFF_HEADS, 128)
    col_s = lambda c0: per_head(proj_s[:, c0:c0 + 1024])
    head_spec = pl.BlockSpec((1, N_DIFF_HEADS, 128), lambda b, g, pt: (b, 0, 0))
    cst = lambda shape: pl.BlockSpec(shape, lambda b, g, pt: (0,) * len(shape))

    def page_spec(t):
        return pl.BlockSpec((1, 1, PAGE_SIZE, N_DIFF_HEADS, 128),
                            lambda b, g, pt: (0, pt[b, g * PAGES_PER_STEP + t], 0, 0, 0))

    page_specs = [page_spec(t) for t in range(PAGES_PER_STEP)]
    n_rows = 2 * N_DIFF_HEADS
    ndh_h = jnp.broadcast_to(norm_diff_head.astype(F32).reshape(1, DIFF_V_DIM) * (1.0 - LAM_INIT),
                             (N_DIFF_HEADS, DIFF_V_DIM))
    merged_s = pl.pallas_call(
        functools.partial(_diff_sample_kernel, n_steps=n_steps),
        grid_spec=pltpu.PrefetchScalarGridSpec(
            num_scalar_prefetch=1,
            grid=(dec_b, n_steps),
            in_specs=[cst((1, DIFF_QK_DIM))] * 4 + [head_spec] * 5 + [cst((N_DIFF_HEADS, DIFF_V_DIM))]
                     + page_specs + page_specs,
            out_specs=head_spec,
            scratch_shapes=[pltpu.VMEM((n_rows, 128), BF16), pltpu.VMEM((n_rows, 1), F32),
                            pltpu.VMEM((n_rows, 1), F32), pltpu.VMEM((n_rows, DIFF_V_DIM), F32)]),
        out_shape=jax.ShapeDtypeStruct((dec_b, N_DIFF_HEADS, DIFF_V_DIM), BF16),
        compiler_params=_params(("parallel", "arbitrary")),
        name="diff_sample",
    )(page_table.astype(jnp.int32), *lams, col_s(C_DQ), col_s(C_DK), col_s(C_DV), col_s(C_GB), per_head(yret_s),
      ndh_h, *([cache_k] * PAGES_PER_STEP), *([cache_v] * PAGES_PER_STEP))

    y_sample = _out_ffn(merged_s.reshape(dec_b, d), xs, w_out_b, g_post, g_fpre, g_fpost, w_gate_b, w_up_b,
                        w_down_b, tm=dec_b).reshape(dec_b, 1, d)

    meta_spec = pl.BlockSpec((N_META, N_DIFF_HEADS, DIFF_V_DIM), lambda b: (0, 0, 0))
    head_rows = pl.BlockSpec((pl.Squeezed(), pl.Squeezed(), N_META, N_DIFF_HEADS, DIFF_V_DIM),
                             lambda b: (0, b, 0, 0, 0))
    any_spec = pl.BlockSpec(memory_space=pl.ANY)
    k_prompt, v_prompt = pl.pallas_call(
        _meta_rows_kernel,
        grid=(bsz,),
        in_specs=[any_spec, any_spec, meta_spec, meta_spec],
        out_specs=[head_rows, head_rows],
        out_shape=[kv_out_shape, kv_out_shape],
        input_output_aliases={0: 0, 1: 1},
        compiler_params=_params(("arbitrary",)),
        name="meta_rows",
    )(kf, vf, kf_m, vf_m)
    k_sample = proj_s[:, C_DK:C_DK + 1024].reshape(1, dec_b, 1, N_DIFF_HEADS, 2 * DIFF_QK_DIM)
    v_sample = proj_s[:, C_DV:C_DV + 1024].reshape(1, dec_b, 1, N_DIFF_HEADS, DIFF_V_DIM)
    return (y_prompt, y_sample, k_prompt, v_prompt, ret_prompt, k_sample, v_sample, ret_sample)
```

```python
import functools

import jax
import jax.numpy as jnp
from jax import lax
from jax.experimental import pallas as pl
from jax.experimental.pallas import tpu as pltpu

F32 = jnp.float32
BF16 = jnp.bfloat16

D_MODEL = 1024
N_META = 16
N_RET_HEADS = 4
RET_QK_DIM = 128
RET_V_DIM = 256
N_DIFF_HEADS = 8
DIFF_QK_DIM = 64
DIFF_V_DIM = 128
PAGE_SIZE = 128
PAST_LEN = 8192
ROPE_BASE = 10000.0
RMS_EPS = 1e-6
LAM_INIT = 0.8 - 0.6 * 1.0
LOG2_E = 1.4426950408889634

C_RQ, C_RK, C_RV, C_RG, C_DQ, C_DK, C_DV, C_GA, C_GB = 0, 512, 1024, 2048, 3072, 4096, 5120, 6144, 7168
D_IN_PROJ = 8192

META_PAD = 128
RET_CHUNK = 256
Q_TILE = 256
Q_SUB = 128
KV_BLOCK = 256
HEADS_PER_STEP = 8
VT_ROWS = DIFF_V_DIM + 16
PAGES_PER_STEP = 8
VMEM_LIMIT = 56 * 1024 * 1024


def _rms(x):
    return x * lax.rsqrt(jnp.mean(x * x, axis=-1, keepdims=True) + RMS_EPS)


def _sigmoid(x):
    return 1.0 / (1.0 + jnp.exp(-x))


def _rope(p, cos, sin, n_heads):
    parts = []
    for hh in range(n_heads):
        r = p[:, hh * 128:(hh + 1) * 128]
        parts.append(r * cos + pltpu.roll(r, 64, 1) * sin)
    return parts


def _vt_tile(v):
    extra = VT_ROWS - DIFF_V_DIM
    ones = jnp.where(lax.broadcasted_iota(jnp.int32, (extra, v.shape[0]), 0) == 0, 1.0, 0.0)
    return jnp.concatenate([v.T, ones], axis=0).astype(BF16)


def _store_per_head(ref, x):
    for hh in range(N_DIFF_HEADS):
        ref[:, hh, :] = x[:, hh * 128:(hh + 1) * 128]


def _lambda(lq1_ref, lk1_ref, lq2_ref, lk2_ref):
    a = jnp.sum(lq1_ref[...] * lk1_ref[...], axis=-1, keepdims=True)
    b = jnp.sum(lq2_ref[...] * lk2_ref[...], axis=-1, keepdims=True)
    return jnp.exp(a) - jnp.exp(b) + LAM_INIT


def _inproj_prompt_kernel(x_ref, g_ref, w_ref, cos_ref, sin_ref, ndh_ref,
                          rq_ref, rk_ref, rv_ref, gret_ref, qt_ref, kb_ref, vt_ref, kf_ref, vf_ref,
                          gdiff_ref, *, tm):
    h = (_rms(x_ref[0]) * g_ref[...]).astype(BF16)

    def proj(c0, n):
        return jnp.dot(h, w_ref[:, c0:c0 + n], preferred_element_type=F32)

    cos = cos_ref[...]
    sin = sin_ref[...]
    rq_ref[0] = jnp.concatenate(_rope(proj(C_RQ, 512), cos, sin, N_RET_HEADS), axis=1).astype(BF16)
    rk = jnp.concatenate(_rope(proj(C_RK, 512), cos, sin, N_RET_HEADS), axis=1)
    rk_ref[0] = (rk * (RET_QK_DIM ** -0.5)).astype(BF16)
    rv_ref[0] = proj(C_RV, 1024).astype(BF16)
    rg = proj(C_RG, 1024)
    ga = proj(C_GA, 1024)
    gret_ref[0] = rg * _sigmoid(rg) * _sigmoid(ga)
    dq = proj(C_DQ, 1024) * (DIFF_QK_DIM ** -0.5 * LOG2_E)
    for hh in range(N_DIFF_HEADS):
        for s in range(tm // Q_SUB):
            qt_ref[0, hh, s] = dq[s * Q_SUB:(s + 1) * Q_SUB, hh * 128:(hh + 1) * 128].T.astype(BF16)
    dk = proj(C_DK, 1024)
    _store_per_head(kf_ref, dk)
    kb_ref[0] = dk.astype(BF16)
    dv = proj(C_DV, 1024)
    _store_per_head(vf_ref, dv)
    for hh in range(N_DIFF_HEADS):
        for s in range(tm // KV_BLOCK):
            vt_ref[0, hh, s] = _vt_tile(dv[s * KV_BLOCK:(s + 1) * KV_BLOCK, hh * 128:(hh + 1) * 128])
    gb = proj(C_GB, 1024)
    gdiff_ref[0] = _sigmoid(gb) * ndh_ref[...]


def _meta_kernel(mt_ref, g_ref, wrk_ref, wrv_ref, wdk_ref, wdv_ref, cos_ref, sin_ref, kdec_ref,
                 kf_ref, vf_ref, kb_ref, vt_ref, s_ref):
    h = (_rms(mt_ref[...]) * g_ref[...]).astype(BF16)
    rk = _rope(jnp.dot(h, wrk_ref[...], preferred_element_type=F32), cos_ref[...], sin_ref[...], N_RET_HEADS)
    rv = jnp.dot(h, wrv_ref[...], preferred_element_type=F32).astype(BF16)
    dk = jnp.dot(h, wdk_ref[...], preferred_element_type=F32)
    dv = jnp.dot(h, wdv_ref[...], preferred_element_type=F32)
    _store_per_head(kf_ref, dk[0:N_META])
    _store_per_head(vf_ref, dv[0:N_META])
    kb_ref[...] = dk.astype(BF16)
    for hh in range(N_DIFF_HEADS):
        vt_ref[hh] = _vt_tile(dv[:, hh * 128:(hh + 1) * 128])
    for hh in range(N_RET_HEADS):
        kd = (rk[hh] * (RET_QK_DIM ** -0.5)) * kdec_ref[hh]
        s_ref[hh] = jnp.dot(kd.T.astype(BF16), rv[:, hh * RET_V_DIM:(hh + 1) * RET_V_DIM],
                            preferred_element_type=F32)


def _meta_rows_kernel(k_hbm, v_hbm, km_ref, vm_ref, k_ref, v_ref):
    del k_hbm, v_hbm
    k_ref[...] = km_ref[...]
    v_ref[...] = vm_ref[...]


def _inproj_sample_kernel(x_ref, g_ref, w_ref, cos_ref, sin_ref, out_ref):
    j = pl.program_id(0)
    h = (_rms(x_ref[...]) * g_ref[...]).astype(BF16)
    p = jnp.dot(h, w_ref[...], preferred_element_type=F32)

    @pl.when(j == 0)
    def _():
        parts = _rope(p, cos_ref[...], sin_ref[...], 2 * N_RET_HEADS)
        parts = parts[:N_RET_HEADS] + [r * (RET_QK_DIM ** -0.5) for r in parts[N_RET_HEADS:]]
        out_ref[...] = jnp.concatenate(parts, axis=1)

    @pl.when(j != 0)
    def _():
        out_ref[...] = p


def _ret_prompt_kernel(rq_ref, rk_ref, rv_ref, gret_ref, smeta_ref, decay_ref, qdec_ref, kdec_ref, gc_ref,
                       y_ref, sout_ref, state_ref, *, n_chunks):
    c = pl.program_id(1)

    @pl.when(c == 0)
    def _():
        state_ref[...] = smeta_ref[...]

    for hh in range(N_RET_HEADS):
        q = rq_ref[0, :, hh * 128:(hh + 1) * 128]
        k = rk_ref[0, :, hh * 128:(hh + 1) * 128]
        v = rv_ref[0, :, hh * RET_V_DIM:(hh + 1) * RET_V_DIM]
        s_old = state_ref[hh]
        sc = lax.dot_general(q, k, (((1,), (1,)), ((), ())), preferred_element_type=F32) * decay_ref[hh]
        o = jnp.dot(sc.astype(BF16), v, preferred_element_type=F32)
        o = o + jnp.dot(q, s_old.astype(BF16), preferred_element_type=F32) * qdec_ref[hh]
        kd = k.astype(F32) * kdec_ref[hh]
        state_ref[hh] = s_old * gc_ref[hh] + jnp.dot(kd.T.astype(BF16), v, preferred_element_type=F32)
        y_ref[0, :, hh * RET_V_DIM:(hh + 1) * RET_V_DIM] = (
            _rms(o) * gret_ref[0, :, hh * RET_V_DIM:(hh + 1) * RET_V_DIM])

    @pl.when(c == n_chunks - 1)
    def _():
        sout_ref[0, 0] = state_ref[...]


def _ret_sample_kernel(rqk_ref, rv_ref, rg_ref, ga_ref, st_ref, gam_ref, y_ref, snew_ref, *, rows):
    eye = (lax.broadcasted_iota(jnp.int32, (128, 128), 0) == lax.broadcasted_iota(jnp.int32, (128, 128), 1))

    def column(r):
        return jnp.sum(jnp.where(eye, jnp.broadcast_to(r, (128, 128)), 0.0), axis=1, keepdims=True)

    for bl in range(rows):
        for hh in range(N_RET_HEADS):
            q = rqk_ref[bl:bl + 1, hh * 128:(hh + 1) * 128]
            k = rqk_ref[bl:bl + 1, 512 + hh * 128:512 + (hh + 1) * 128]
            cs = slice(hh * RET_V_DIM, (hh + 1) * RET_V_DIM)
            v = rv_ref[bl:bl + 1, cs]
            s_old = st_ref[0, bl, hh]
            gam = gam_ref[hh]
            score = jnp.sum(q * k, axis=1, keepdims=True)
            o = score * v + jnp.sum(column(q) * s_old, axis=0, keepdims=True) * gam
            snew_ref[0, bl, hh] = s_old * gam + column(k) * v
            rg = rg_ref[bl:bl + 1, cs]
            y_ref[bl:bl + 1, cs] = _rms(o) * (rg * _sigmoid(rg)) * _sigmoid(ga_ref[bl:bl + 1, cs])


def _diff_prompt_kernel(lq1_ref, lk1_ref, lq2_ref, lk2_ref, qt_ref, k_ref, vt_ref, km_ref, vtm_ref,
                        gdiff_ref, yret_ref, out_ref, qbd_scr, m_scr, acc_scr):
    qi = pl.program_id(2)
    wide = 2 * Q_SUB
    row = lax.broadcasted_iota(jnp.int32, (128, wide), 0)
    col = lax.broadcasted_iota(jnp.int32, (128, wide), 1)
    bd_mask = (row < DIFF_QK_DIM) == (col < Q_SUB)
    n_sub = Q_TILE // Q_SUB
    units = [(hh, st) for hh in range(HEADS_PER_STEP) for st in range(n_sub)]

    def scores(keys):
        return [jnp.dot(kb, qbd_scr[u], preferred_element_type=F32) for u, kb in enumerate(keys)]

    def softmax_pv(ss, vts):
        probs = []
        for u, s in enumerate(ss):
            m_old = m_scr[u]
            m_new = jnp.maximum(m_old, jnp.max(s, axis=0, keepdims=True))
            alpha = jnp.exp2(m_old - m_new)
            p = jnp.exp2(s - m_new)
            m_scr[u] = m_new
            probs.append((alpha, p.astype(BF16)))
        for u, (vt, (alpha, p)) in enumerate(zip(vts, probs)):
            acc_scr[u] = acc_scr[u] * alpha + jnp.dot(vt, p, preferred_element_type=F32)

    def key_block(j):
        off = pl.multiple_of(j * KV_BLOCK, KV_BLOCK)
        return [k_ref[0, pl.ds(off, KV_BLOCK), hh * 128:(hh + 1) * 128] for hh, _ in units]

    for u, (hh, st) in enumerate(units):
        qt = qt_ref[0, hh, st]
        qbd_scr[u] = jnp.where(bd_mask, jnp.concatenate([qt, qt], axis=1), jnp.zeros((), BF16))
    m_scr[...] = jnp.full(m_scr.shape, -jnp.inf, F32)
    acc_scr[...] = jnp.zeros(acc_scr.shape, F32)

    softmax_pv(scores([km_ref[0:N_META, hh * 128:(hh + 1) * 128] for hh, _ in units]),
               [vtm_ref[hh][:, 0:N_META] for hh, _ in units])

    def body(j, carry):
        softmax_pv(scores(key_block(j)), [vt_ref[0, hh, j] for hh, _ in units])
        return carry

    lax.fori_loop(0, qi, body, 0)

    def visible(nk):
        kpos = lax.broadcasted_iota(jnp.int32, (nk, wide), 0)
        qpos = (lax.broadcasted_iota(jnp.int32, (nk, wide), 1) & (Q_SUB - 1)) + (nk - Q_SUB)
        return kpos <= qpos

    off = pl.multiple_of(qi * KV_BLOCK, KV_BLOCK)
    ends = [Q_SUB * (st + 1) for _, st in units]
    diag = scores([k_ref[0, pl.ds(off, nk), hh * 128:(hh + 1) * 128] for (hh, _), nk in zip(units, ends)])
    softmax_pv([jnp.where(visible(nk), s, -jnp.inf) for s, nk in zip(diag, ends)],
               [vt_ref[0, hh, qi][:, 0:nk] for (hh, _), nk in zip(units, ends)])

    lam = _lambda(lq1_ref, lk1_ref, lq2_ref, lk2_ref)
    for u, (hh, st) in enumerate(units):
        hs = slice(hh * 128, (hh + 1) * 128)
        o = acc_scr[u, 0:DIFF_V_DIM] * (1.0 / acc_scr[u, DIFF_V_DIM:DIFF_V_DIM + 1])
        od = o[:, :Q_SUB] - lam * o[:, Q_SUB:]
        on = od * lax.rsqrt(jnp.mean(od * od, axis=0, keepdims=True) + RMS_EPS)
        rs = slice(st * Q_SUB, (st + 1) * Q_SUB)
        out_ref[0, rs, hs] = (yret_ref[0, rs, hs] + on.T * gdiff_ref[0, rs, hs]).astype(BF16)


def _diff_sample_kernel(pt_ref, lq1_ref, lk1_ref, lq2_ref, lk2_ref, q_ref, kn_ref, vn_ref, gb_ref,
                        yret_ref, ndh_ref, *refs, n_steps):
    del pt_ref
    k_refs = refs[:PAGES_PER_STEP]
    v_refs = refs[PAGES_PER_STEP:2 * PAGES_PER_STEP]
    out_ref, qall_ref, m_ref, l_ref, acc_ref = refs[2 * PAGES_PER_STEP:]
    g = pl.program_id(1)
    n_rows = 2 * N_DIFF_HEADS
    n_kh = PAGE_SIZE * N_DIFF_HEADS

    @pl.when(g == 0)
    def _():
        q = q_ref[0] * (DIFF_QK_DIM ** -0.5)
        lane = lax.broadcasted_iota(jnp.int32, q.shape, 1)
        qall = jnp.concatenate([jnp.where(lane < DIFF_QK_DIM, q, 0.0),
                                jnp.where(lane >= DIFF_QK_DIM, q, 0.0)], axis=0)
        qall_ref[...] = qall.astype(BF16)
        kn = kn_ref[0]
        vn = vn_ref[0]
        m_ref[...] = jnp.sum(qall * jnp.concatenate([kn, kn], axis=0), axis=1, keepdims=True)
        l_ref[...] = jnp.ones_like(l_ref)
        acc_ref[...] = jnp.concatenate([vn, vn], axis=0)

    qall = qall_ref[...]
    own_head = ((lax.broadcasted_iota(jnp.int32, (n_rows, n_kh), 1) & (N_DIFF_HEADS - 1))
                == (lax.broadcasted_iota(jnp.int32, (n_rows, n_kh), 0) & (N_DIFF_HEADS - 1)))
    s_pages = []
    for t in range(PAGES_PER_STEP):
        k2 = k_refs[t][0, 0].reshape(n_kh, 128).astype(BF16)
        s = lax.dot_general(qall, k2, (((1,), (1,)), ((), ())), preferred_element_type=F32)
        s_pages.append(jnp.where(own_head, s, -jnp.inf))
    s = jnp.concatenate(s_pages, axis=1)
    m_old = m_ref[...]
    m_new = jnp.maximum(m_old, jnp.max(s, axis=1, keepdims=True))
    alpha = jnp.exp(m_old - m_new)
    p = jnp.exp(s - m_new)
    l_ref[...] = alpha * l_ref[...] + jnp.sum(p, axis=1, keepdims=True)
    p = p.astype(BF16)
    pv = None
    for t in range(PAGES_PER_STEP):
        v2 = v_refs[t][0, 0].reshape(n_kh, 128).astype(BF16)
        d = jnp.dot(p[:, t * n_kh:(t + 1) * n_kh], v2, preferred_element_type=F32)
        pv = d if pv is None else pv + d
    acc_ref[...] = acc_ref[...] * alpha + pv
    m_ref[...] = m_new

    @pl.when(g == n_steps - 1)
    def _():
        lam = _lambda(lq1_ref, lk1_ref, lq2_ref, lk2_ref)
        o = acc_ref[...] * (1.0 / l_ref[...])
        od = o[0:N_DIFF_HEADS] - lam * o[N_DIFF_HEADS:]
        y = _rms(od) * _sigmoid(gb_ref[0]) * ndh_ref[...]
        out_ref[0] = (yret_ref[0] + y).astype(BF16)


def _out_ffn_kernel(mg_ref, x_ref, wo_ref, gpost_ref, gpre_ref, gffn_ref, wg_ref, wu_ref, wd_ref, y_ref):
    mix = jnp.dot(mg_ref[...], wo_ref[...], preferred_element_type=F32)
    x1 = x_ref[...] + _rms(mix) * gpost_ref[...]
    h2 = (_rms(x1) * gpre_ref[...]).astype(BF16)
    a = jnp.dot(h2, wg_ref[...], preferred_element_type=F32)
    u = jnp.dot(h2, wu_ref[...], preferred_element_type=F32)
    act = (a * _sigmoid(a) * u).astype(BF16)
    ff = jnp.dot(act, wd_ref[...], preferred_element_type=F32)
    y_ref[...] = x1 + _rms(ff) * gffn_ref[...]


def _const_spec(shape):
    nd = len(shape)
    return pl.BlockSpec(shape, lambda *_: (0,) * nd, pipeline_mode=pl.Buffered(1))


def _params(sem):
    return pltpu.CompilerParams(dimension_semantics=sem, vmem_limit_bytes=VMEM_LIMIT)


def _rope_tables(pos):
    half = RET_QK_DIM // 2
    freqs = ROPE_BASE ** (-jnp.arange(half, dtype=F32) / half)
    ang = pos[:, None] * freqs[None, :]
    cos, sin = jnp.cos(ang), jnp.sin(ang)
    return jnp.concatenate([cos, cos], axis=1), jnp.concatenate([-sin, sin], axis=1)


def _log_gamma():
    return jnp.log(1.0 - 2.0 ** (-5.0 - jnp.arange(N_RET_HEADS, dtype=F32)))


def _out_ffn(merged, x, w_out, g_post, g_pre, g_ffn, w_gate, w_up, w_down, tm):
    n, d = x.shape
    d_ff = w_gate.shape[1]
    row = lambda i: (i, 0)
    return pl.pallas_call(
        _out_ffn_kernel,
        grid=(n // tm,),
        in_specs=[pl.BlockSpec((tm, d), row), pl.BlockSpec((tm, d), row), _const_spec((d, d)),
                  _const_spec((1, d)), _const_spec((1, d)), _const_spec((1, d)),
                  _const_spec((d, d_ff)), _const_spec((d, d_ff)), _const_spec((d_ff, d))],
        out_specs=pl.BlockSpec((tm, d), row),
        out_shape=jax.ShapeDtypeStruct((n, d), F32),
        compiler_params=_params(("parallel",)),
        name="out_ffn",
    )(merged, x, w_out, g_post, g_pre, g_ffn, w_gate, w_up, w_down)


def kernel(x_prompt, x_sample, cache_k, cache_v, state_ret, page_table, meta_tokens, norm_mix_pre,
           norm_mix_post, w_in, lambda_q1, lambda_k1, lambda_q2, lambda_k2, norm_diff_head, w_out,
           norm_ffn_pre, norm_ffn_post, w_gate, w_up, w_down):
    bsz, seq, d = x_prompt.shape
    dec_b, t_s, _ = x_sample.shape
    depth, n_pool = cache_k.shape[0], cache_k.shape[1]
    n_pages = page_table.shape[1]
    assert depth == 1 and t_s == 1 and d == D_MODEL and w_in.shape[2] == D_IN_PROJ
    assert seq % Q_TILE == 0 and n_pages % PAGES_PER_STEP == 0

    w_in_b = w_in[0].astype(BF16)
    w_out_b = w_out[0].astype(BF16)
    w_gate_b, w_up_b, w_down_b = w_gate[0].astype(BF16), w_up[0].astype(BF16), w_down[0].astype(BF16)
    g_pre = norm_mix_pre.astype(F32).reshape(1, d)
    g_post = norm_mix_post.astype(F32).reshape(1, d)
    g_fpre = norm_ffn_pre.astype(F32).reshape(1, d)
    g_fpost = norm_ffn_post.astype(F32).reshape(1, d)
    ndh = (jnp.tile(norm_diff_head.astype(F32).reshape(1, DIFF_V_DIM), (1, N_DIFF_HEADS)) * (1.0 - LAM_INIT))
    lams = [a.astype(F32).reshape(1, DIFF_QK_DIM) for a in (lambda_q1, lambda_k1, lambda_q2, lambda_k2)]
    lam_specs = [_const_spec((1, DIFF_QK_DIM))] * 4

    lg = _log_gamma()
    cos_p, sin_p = _rope_tables(jnp.arange(seq + N_META, dtype=F32))
    cos_m = jnp.pad(cos_p[:N_META], ((0, META_PAD - N_META), (0, 0)))
    sin_m = jnp.pad(sin_p[:N_META], ((0, META_PAD - N_META), (0, 0)))
    cos_r, sin_r = cos_p[N_META:], sin_p[N_META:]
    cos_s, sin_s = _rope_tables(PAST_LEN + jnp.arange(1, dtype=F32))

    ci = jnp.arange(RET_CHUNK, dtype=F32)
    diff = ci[:, None] - ci[None, :]
    decay = jnp.where(diff >= 0, jnp.exp(jnp.maximum(diff, 0.0)[None] * lg[:, None, None]), 0.0)
    qdec = jnp.broadcast_to(jnp.exp((ci + 1.0)[None, :] * lg[:, None])[:, :, None],
                            (N_RET_HEADS, RET_CHUNK, RET_V_DIM))
    kdec = jnp.broadcast_to(jnp.exp((RET_CHUNK - 1.0 - ci)[None, :] * lg[:, None])[:, :, None],
                            (N_RET_HEADS, RET_CHUNK, RET_QK_DIM))
    gchunk = jnp.broadcast_to(jnp.exp(RET_CHUNK * lg)[:, None, None], (N_RET_HEADS, 1, RET_V_DIM))
    mi = jnp.arange(META_PAD, dtype=F32)
    kdec_m = jnp.where(mi[None, :] < N_META, jnp.exp((N_META - 1.0 - mi)[None, :] * lg[:, None]), 0.0)
    kdec_m = jnp.broadcast_to(kdec_m[:, :, None], (N_RET_HEADS, META_PAD, RET_QK_DIM))
    gam1 = jnp.broadcast_to(jnp.exp(lg)[:, None, None], (N_RET_HEADS, 1, RET_V_DIM))

    mt = jnp.pad(meta_tokens.astype(F32), ((0, META_PAD - N_META), (0, 0)))
    kf_m, vf_m, kb_m, vt_m, s_meta = pl.pallas_call(
        _meta_kernel,
        grid=(1,),
        in_specs=[_const_spec((META_PAD, d)), _const_spec((1, d)),
                  pl.BlockSpec((d, 512), lambda i: (0, C_RK // 512)),
                  pl.BlockSpec((d, 1024), lambda i: (0, C_RV // 1024)),
                  pl.BlockSpec((d, 1024), lambda i: (0, C_DK // 1024)),
                  pl.BlockSpec((d, 1024), lambda i: (0, C_DV // 1024)),
                  _const_spec((META_PAD, 128)), _const_spec((META_PAD, 128)),
                  _const_spec((N_RET_HEADS, META_PAD, RET_QK_DIM))],
        out_specs=[pl.BlockSpec((N_META, N_DIFF_HEADS, 128), lambda i: (0, 0, 0)),
                   pl.BlockSpec((N_META, N_DIFF_HEADS, 128), lambda i: (0, 0, 0)),
                   pl.BlockSpec((META_PAD, d), lambda i: (0, 0)),
                   pl.BlockSpec((N_DIFF_HEADS, VT_ROWS, META_PAD), lambda i: (0, 0, 0)),
                   pl.BlockSpec((N_RET_HEADS, RET_QK_DIM, RET_V_DIM), lambda i: (0, 0, 0))],
        out_shape=[jax.ShapeDtypeStruct((N_META, N_DIFF_HEADS, 128), F32),
                   jax.ShapeDtypeStruct((N_META, N_DIFF_HEADS, 128), F32),
                   jax.ShapeDtypeStruct((META_PAD, d), BF16),
                   jax.ShapeDtypeStruct((N_DIFF_HEADS, VT_ROWS, META_PAD), BF16),
                   jax.ShapeDtypeStruct((N_RET_HEADS, RET_QK_DIM, RET_V_DIM), F32)],
        compiler_params=_params(("arbitrary",)),
        name="meta_proj",
    )(mt, g_pre, w_in_b, w_in_b, w_in_b, w_in_b, cos_m, sin_m, kdec_m)

    tm = 256
    n_t = seq // tm
    tile3 = lambda b, i: (b, i, 0)
    kv_out_shape = jax.ShapeDtypeStruct((1, bsz, seq + N_META, N_DIFF_HEADS, DIFF_V_DIM), F32)
    kv_out_spec = pl.BlockSpec(
        (pl.Squeezed(), pl.Squeezed(), pl.Element(tm), pl.Element(N_DIFF_HEADS), pl.Element(DIFF_V_DIM)),
        lambda b, i: (0, b, N_META + i * tm, 0, 0))
    rq, rk, rv, gret, qt, kb, vt, kf, vf, gdiff = pl.pallas_call(
        functools.partial(_inproj_prompt_kernel, tm=tm),
        grid=(bsz, n_t),
        in_specs=[pl.BlockSpec((1, tm, d), tile3), _const_spec((1, d)), _const_spec((d, D_IN_PROJ)),
                  pl.BlockSpec((tm, 128), lambda b, i: (i, 0)), pl.BlockSpec((tm, 128), lambda b, i: (i, 0)),
                  _const_spec((1, d))],
        out_specs=[pl.BlockSpec((1, tm, 512), tile3), pl.BlockSpec((1, tm, 512), tile3),
                   pl.BlockSpec((1, tm, d), tile3), pl.BlockSpec((1, tm, d), tile3),
                   pl.BlockSpec((1, N_DIFF_HEADS, tm // Q_SUB, 128, Q_SUB), lambda b, i: (b, 0, i, 0, 0)),
                   pl.BlockSpec((1, tm, d), tile3),
                   pl.BlockSpec((1, N_DIFF_HEADS, tm // KV_BLOCK, VT_ROWS, KV_BLOCK), lambda b, i: (b, 0, i, 0, 0)),
                   kv_out_spec, kv_out_spec,
                   pl.BlockSpec((1, tm, d), tile3)],
        out_shape=[jax.ShapeDtypeStruct((bsz, seq, 512), BF16), jax.ShapeDtypeStruct((bsz, seq, 512), BF16),
                   jax.ShapeDtypeStruct((bsz, seq, d), BF16), jax.ShapeDtypeStruct((bsz, seq, d), F32),
                   jax.ShapeDtypeStruct((bsz, N_DIFF_HEADS, seq // Q_SUB, 128, Q_SUB), BF16),
                   jax.ShapeDtypeStruct((bsz, seq, d), BF16),
                   jax.ShapeDtypeStruct((bsz, N_DIFF_HEADS, seq // KV_BLOCK, VT_ROWS, KV_BLOCK), BF16),
                   kv_out_shape, kv_out_shape,
                   jax.ShapeDtypeStruct((bsz, seq, d), F32)],
        compiler_params=_params(("parallel", "parallel")),
        name="inproj_prompt",
    )(x_prompt, g_pre, w_in_b, cos_r, sin_r, ndh)

    n_chunks = seq // RET_CHUNK
    yret, ret_prompt = pl.pallas_call(
        functools.partial(_ret_prompt_kernel, n_chunks=n_chunks),
        grid=(bsz, n_chunks),
        in_specs=[pl.BlockSpec((1, RET_CHUNK, 512), tile3), pl.BlockSpec((1, RET_CHUNK, 512), tile3),
                  pl.BlockSpec((1, RET_CHUNK, d), tile3), pl.BlockSpec((1, RET_CHUNK, d), tile3),
                  _const_spec((N_RET_HEADS, RET_QK_DIM, RET_V_DIM)),
                  _const_spec((N_RET_HEADS, RET_CHUNK, RET_CHUNK)),
                  _const_spec((N_RET_HEADS, RET_CHUNK, RET_V_DIM)),
                  _const_spec((N_RET_HEADS, RET_CHUNK, RET_QK_DIM)),
                  _const_spec((N_RET_HEADS, 1, RET_V_DIM))],
        out_specs=[pl.BlockSpec((1, RET_CHUNK, d), tile3),
                   pl.BlockSpec((1, 1, N_RET_HEADS, RET_QK_DIM, RET_V_DIM), lambda b, c: (0, b, 0, 0, 0))],
        out_shape=[jax.ShapeDtypeStruct((bsz, seq, d), F32),
                   jax.ShapeDtypeStruct((1, bsz, N_RET_HEADS, RET_QK_DIM, RET_V_DIM), F32)],
        scratch_shapes=[pltpu.VMEM((N_RET_HEADS, RET_QK_DIM, RET_V_DIM), F32)],
        compiler_params=_params(("parallel", "arbitrary")),
        name="ret_prompt",
    )(rq, rk, rv, gret, s_meta, decay, qdec, kdec, gchunk)

    n_q = seq // Q_TILE
    hps = HEADS_PER_STEP
    n_units = hps * (Q_TILE // Q_SUB)
    merged = pl.pallas_call(
        _diff_prompt_kernel,
        grid=(bsz, N_DIFF_HEADS // hps, n_q),
        in_specs=lam_specs + [
            pl.BlockSpec((1, hps, Q_TILE // Q_SUB, 128, Q_SUB), lambda b, h, i: (b, h, i, 0, 0)),
            pl.BlockSpec((1, seq, hps * 128), lambda b, h, i: (b, 0, h)),
            pl.BlockSpec((1, hps, seq // KV_BLOCK, VT_ROWS, KV_BLOCK), lambda b, h, i: (b, h, 0, 0, 0)),
            pl.BlockSpec((META_PAD, hps * 128), lambda b, h, i: (0, h)),
            pl.BlockSpec((hps, VT_ROWS, META_PAD), lambda b, h, i: (h, 0, 0)),
            pl.BlockSpec((1, Q_TILE, hps * 128), lambda b, h, i: (b, i, h)),
            pl.BlockSpec((1, Q_TILE, hps * 128), lambda b, h, i: (b, i, h))],
        out_specs=pl.BlockSpec((1, Q_TILE, hps * 128), lambda b, h, i: (b, i, h)),
        out_shape=jax.ShapeDtypeStruct((bsz, seq, d), BF16),
        scratch_shapes=[pltpu.VMEM((n_units, 128, 2 * Q_SUB), BF16), pltpu.VMEM((n_units, 1, 2 * Q_SUB), F32),
                        pltpu.VMEM((n_units, VT_ROWS, 2 * Q_SUB), F32)],
        compiler_params=_params(("parallel", "parallel", "arbitrary")),
        name="diff_prompt",
    )(*lams, qt, kb, vt, kb_m, vt_m, gdiff, yret)

    y_prompt = _out_ffn(merged.reshape(bsz * seq, d), x_prompt.reshape(bsz * seq, d), w_out_b, g_post,
                        g_fpre, g_fpost, w_gate_b, w_up_b, w_down_b, tm=512).reshape(bsz, seq, d)

    xs = x_sample.reshape(dec_b, d)
    proj_s = pl.pallas_call(
        _inproj_sample_kernel,
        grid=(D_IN_PROJ // 1024,),
        in_specs=[_const_spec((dec_b, d)), _const_spec((1, d)), pl.BlockSpec((d, 1024), lambda j: (0, j)),
                  _const_spec((1, 128)), _const_spec((1, 128))],
        out_specs=pl.BlockSpec((dec_b, 1024), lambda j: (0, j)),
        out_shape=jax.ShapeDtypeStruct((dec_b, D_IN_PROJ), F32),
        compiler_params=_params(("parallel",)),
        name="inproj_sample",
    )(xs, g_pre, w_in_b, cos_s, sin_s)

    rows = 8
    st_spec = pl.BlockSpec((1, rows, N_RET_HEADS, RET_QK_DIM, RET_V_DIM), lambda g: (0, g, 0, 0, 0))
    colblk = lambda j: pl.BlockSpec((rows, 1024), lambda g: (g, j))
    yret_s, ret_sample = pl.pallas_call(
        functools.partial(_ret_sample_kernel, rows=rows),
        grid=(dec_b // rows,),
        in_specs=[colblk(0), colblk(C_RV // 1024), colblk(C_RG // 1024), colblk(C_GA // 1024), st_spec,
                  _const_spec((N_RET_HEADS, 1, RET_V_DIM))],
        out_specs=[pl.BlockSpec((rows, d), lambda g: (g, 0)), st_spec],
        out_shape=[jax.ShapeDtypeStruct((dec_b, d), F32),
                   jax.ShapeDtypeStruct((1, dec_b, N_RET_HEADS, RET_QK_DIM, RET_V_DIM), F32)],
        compiler_params=_params(("parallel",)),
        name="ret_sample",
    )(proj_s, proj_s, proj_s, proj_s, state_ret.astype(F32), gam1)

    n_steps = n_pages // PAGES_PER_STEP
    per_head = lambda a: a.reshape(dec_b, N_DIFF_HEADS, 128)
    col_s = lambda c0: per_head(proj_s[:, c0:c0 + 1024])
    head_spec = pl.BlockSpec((1, N_DIFF_HEADS, 128), lambda b, g, pt: (b, 0, 0))
    cst = lambda shape: pl.BlockSpec(shape, lambda b, g, pt: (0,) * len(shape))

    def page_spec(t):
        return pl.BlockSpec((1, 1, PAGE_SIZE, N_DIFF_HEADS, 128),
                            lambda b, g, pt: (0, pt[b, g * PAGES_PER_STEP + t], 0, 0, 0))

    page_specs = [page_spec(t) for t in range(PAGES_PER_STEP)]
    n_rows = 2 * N_DIFF_HEADS
    ndh_h = jnp.broadcast_to(norm_diff_head.astype(F32).reshape(1, DIFF_V_DIM) * (1.0 - LAM_INIT),
                             (N_DIFF_HEADS, DIFF_V_DIM))
    merged_s = pl.pallas_call(
        functools.partial(_diff_sample_kernel, n_steps=n_steps),
        grid_spec=pltpu.PrefetchScalarGridSpec(
            num_scalar_prefetch=1,
            grid=(dec_b, n_steps),
            in_specs=[cst((1, DIFF_QK_DIM))] * 4 + [head_spec] * 5 + [cst((N_DIFF_HEADS, DIFF_V_DIM))]
                     + page_specs + page_specs,
            out_specs=head_spec,
            scratch_shapes=[pltpu.VMEM((n_rows, 128), BF16), pltpu.VMEM((n_rows, 1), F32),
                            pltpu.VMEM((n_rows, 1), F32), pltpu.VMEM((n_rows, DIFF_V_DIM), F32)]),
        out_shape=jax.ShapeDtypeStruct((dec_b, N_DIFF_HEADS, DIFF_V_DIM), BF16),
        compiler_params=_params(("parallel", "arbitrary")),
        name="diff_sample",
    )(page_table.astype(jnp.int32), *lams, col_s(C_DQ), col_s(C_DK), col_s(C_DV), col_s(C_GB), per_head(yret_s),
      ndh_h, *([cache_k] * PAGES_PER_STEP), *([cache_v] * PAGES_PER_STEP))

    y_sample = _out_ffn(merged_s.reshape(dec_b, d), xs, w_out_b, g_post, g_fpre, g_fpost, w_gate_b, w_up_b,
                        w_down_b, tm=dec_b).reshape(dec_b, 1, d)

    meta_spec = pl.BlockSpec((N_META, N_DIFF_HEADS, DIFF_V_DIM), lambda b: (0, 0, 0))
    head_rows = pl.BlockSpec((pl.Squeezed(), pl.Squeezed(), N_META, N_DIFF_HEADS, DIFF_V_DIM),
                             lambda b: (0, b, 0, 0, 0))
    any_spec = pl.BlockSpec(memory_space=pl.ANY)
    k_prompt, v_prompt = pl.pallas_call(
        _meta_rows_kernel,
        grid=(bsz,),
        in_specs=[any_spec, any_spec, meta_spec, meta_spec],
        out_specs=[head_rows, head_rows],
        out_shape=[kv_out_shape, kv_out_shape],
        input_output_aliases={0: 0, 1: 1},
        compiler_params=_params(("arbitrary",)),
        name="meta_rows",
    )(kf, vf, kf_m, vf_m)
    k_sample = proj_s[:, C_DK:C_DK + 1024].reshape(1, dec_b, 1, N_DIFF_HEADS, 2 * DIFF_QK_DIM)
    v_sample = proj_s[:, C_DV:C_DV + 1024].reshape(1, dec_b, 1, N_DIFF_HEADS, DIFF_V_DIM)
    return (y_prompt, y_sample, k_prompt, v_prompt, ret_prompt, k_sample, v_sample, ret_sample)
```
